```python
import jax, jax.numpy as jnp
from jax import lax
import numpy as np

D_MODEL = 2048
BATCH = 4
SEQ = 8192
DEPTH = 4

N_MIXERS = 2
EPS = 1e-6

A_HEAD_DIM = 128
A_HEADS = D_MODEL // A_HEAD_DIM
A_WIDTH = A_HEADS * A_HEAD_DIM
A_PATTERNS = ((128, 1), (512, 4), (2048, 16))
A_N_GROUPS = len(A_PATTERNS)
A_PROJ = A_N_GROUPS * 3 * A_WIDTH
A_BLOCK = 128
ROPE_THETA = 10000.0

R_QK_DIM = 256
R_V_DIM = 512
R_HEADS = D_MODEL // R_QK_DIM
R_QK_WIDTH = R_HEADS * R_QK_DIM
R_V_WIDTH = R_HEADS * R_V_DIM
R_PROJ = 2 * R_QK_WIDTH + 2 * R_V_WIDTH
R_CHUNK = 128

D_FF = 5632
CONV_WIDTH = 3

N_ATTN_LAYERS = (DEPTH + N_MIXERS - 1) // N_MIXERS
N_RET_LAYERS = DEPTH // N_MIXERS

kernel_name = 'hybrid_dilated_swa_retention_convffn'


def rmsnorm(x, g):
    xf = x.astype(jnp.float32)
    y = xf * lax.rsqrt(jnp.mean(xf * xf, axis=-1, keepdims=True) + EPS)
    return (y * g.astype(jnp.float32)).astype(x.dtype)


def rotary_tables(seq_len, inv_freq):
    ang = jnp.arange(seq_len, dtype=jnp.float32)[:, None] * inv_freq[None, :]
    return jnp.cos(ang), jnp.sin(ang)


def rotate(x, cos, sin):
    x1, x2 = jnp.split(x.astype(jnp.float32), 2, axis=-1)
    c, s = cos[None, :, None, :], sin[None, :, None, :]
    return jnp.concatenate([x1 * c - x2 * s, x1 * s + x2 * c], axis=-1).astype(x.dtype)


def dilated_window_group(q, k, v, window, dilation):
    B, S, H, Dh = q.shape
    n_back = window // dilation
    assert n_back <= A_BLOCK
    span = dilation * A_BLOCK
    s_pad = -(-S // span) * span
    L = s_pad // dilation
    nb = L // A_BLOCK
    pad = ((0, 0), (0, s_pad - S), (0, 0), (0, 0))

    def to_blocks(t):
        t = jnp.pad(t, pad).reshape(B, L, dilation, H, Dh).transpose(0, 2, 1, 3, 4)
        return t.reshape(B, dilation, nb, A_BLOCK, H, Dh)

    def with_prev(t):
        prev = jnp.pad(t, ((0, 0), (0, 0), (1, 0), (0, 0), (0, 0), (0, 0)))[:, :, :-1]
        return jnp.concatenate([prev, t], axis=3)

    qb = to_blocks(q)
    kb = with_prev(to_blocks(k))
    vb = with_prev(to_blocks(v))
    s = jnp.einsum('brnihd,brnjhd->brnhij', qb, kb,
                   preferred_element_type=jnp.float32) * (Dh ** -0.5)
    i = jnp.arange(A_BLOCK)[:, None]
    j = jnp.arange(2 * A_BLOCK)[None, :]
    dist = i + A_BLOCK - j
    band = (dist >= 0) & (dist <= n_back)
    has_prev = (jnp.arange(nb) > 0)[:, None, None]
    valid = band[None] & (has_prev | (j >= A_BLOCK)[None])
    s = jnp.where(valid[None, None, :, None], s, -jnp.inf)
    m = jnp.max(s, axis=-1, keepdims=True)
    p = jnp.exp(s - m)
    l = jnp.sum(p, axis=-1, keepdims=True)
    o = jnp.einsum('brnhij,brnjhd->brnihd', p / l, vb.astype(jnp.float32))
    lse = (m + jnp.log(l))[..., 0].transpose(0, 1, 2, 4, 3)
    o = o.reshape(B, dilation, L, H, Dh).transpose(0, 2, 1, 3, 4).reshape(B, s_pad, H, Dh)[:, :S]
    lse = lse.reshape(B, dilation, L, H).transpose(0, 2, 1, 3).reshape(B, s_pad, H)[:, :S]
    return o, lse


def dilated_attention_mixer(h, w_in, w_out, cos, sin):
    B, S, _ = h.shape
    proj = (h @ w_in).reshape(B, S, A_N_GROUPS, 3, A_HEADS, A_HEAD_DIM)
    outs, lses = [], []
    for g, (window, dilation) in enumerate(A_PATTERNS):
        q = rotate(proj[:, :, g, 0], cos, sin)
        k = rotate(proj[:, :, g, 1], cos, sin)
        o, lse = dilated_window_group(q, k, proj[:, :, g, 2], window, dilation)
        outs.append(o)
        lses.append(lse)
    wts = jax.nn.softmax(jnp.stack(lses), axis=0)
    o = jnp.einsum('gbsh,gbshd->bshd', wts, jnp.stack(outs))
    return o.reshape(B, S, A_WIDTH).astype(h.dtype) @ w_out


def chunkwise_retention(q, k, v):
    B, S, H, dk = q.shape
    dv = v.shape[-1]
    C = R_CHUNK
    N = S // C
    log_gamma = jnp.log1p(-jnp.exp2(-5.0 - jnp.arange(H, dtype=jnp.float32)))
    idx = jnp.arange(C, dtype=jnp.float32)
    rel = idx[:, None] - idx[None, :]
    decay = jnp.where(rel >= 0, jnp.exp(log_gamma[:, None, None] * jnp.maximum(rel, 0.0)), 0.0)
    q_decay = jnp.exp(log_gamma[None, :] * (idx[:, None] + 1.0))
    k_decay = jnp.exp(log_gamma[None, :] * (C - 1.0 - idx[:, None]))
    chunk_decay = jnp.exp(log_gamma * C)
    qc = q.astype(jnp.float32).reshape(B, N, C, H, dk)
    kc = k.astype(jnp.float32).reshape(B, N, C, H, dk)
    vc = v.astype(jnp.float32).reshape(B, N, C, H, dv)
    scores = jnp.einsum('bnihd,bnjhd->bnhij', qc, kc) * decay[None, None]
    y_intra = jnp.einsum('bnhij,bnjhe->bnihe', scores, vc)

    def step(state, inp):
        q_n, k_n, v_n = inp
        cross = jnp.einsum('bihd,bhde->bihe', q_n * q_decay[None, :, :, None], state)
        state = state * chunk_decay[None, :, None, None] + jnp.einsum(
            'bjhd,bjhe->bhde', k_n * k_decay[None, :, :, None], v_n)
        return state, cross

    state0 = jnp.zeros((B, H, dk, dv), jnp.float32)
    _, y_cross = lax.scan(step, state0, (qc.swapaxes(0, 1), kc.swapaxes(0, 1), vc.swapaxes(0, 1)))
    return (y_intra + y_cross.swapaxes(0, 1)).reshape(B, S, H, dv)


def retention_mixer(h, w_in, w_out, cos, sin):
    B, S, _ = h.shape
    proj = h @ w_in
    q, k, v, g = jnp.split(proj, [R_QK_WIDTH, 2 * R_QK_WIDTH, 2 * R_QK_WIDTH + R_V_WIDTH], axis=-1)
    q = rotate(q.reshape(B, S, R_HEADS, R_QK_DIM), cos, sin)
    k = rotate(k.reshape(B, S, R_HEADS, R_QK_DIM), cos, sin) * (R_QK_DIM ** -0.5)
    y = chunkwise_retention(q, k, v.reshape(B, S, R_HEADS, R_V_DIM))
    mu = jnp.mean(y, axis=-1, keepdims=True)
    var = jnp.mean(jnp.square(y - mu), axis=-1, keepdims=True)
    y = (y - mu) * lax.rsqrt(var + EPS)
    out = jax.nn.silu(g.astype(jnp.float32)) * y.reshape(B, S, R_V_WIDTH)
    return out.astype(h.dtype) @ w_out


def conv_ffn(h, w_up, conv_w, conv_b, w_down):
    u = h @ w_up
    c = u.shape[-1]
    u = lax.conv_general_dilated(
        u, conv_w[:, None, :].astype(u.dtype), window_strides=(1,),
        padding=[(CONV_WIDTH - 1, 0)], dimension_numbers=('NWC', 'WIO', 'NWC'),
        feature_group_count=c) + conv_b
    gate, up = jnp.split(u, 2, axis=-1)
    return (jax.nn.silu(gate) * up) @ w_down


def setup_inputs(seed: int = 0) -> dict:
    key = jax.random.key(seed)
    ks = jax.random.split(key, 12)
    f32 = jnp.float32
    out_scale = (2.0 * DEPTH) ** -0.5

    def w(k, shape, fan_in, scale=1.0):
        return jax.random.normal(k, shape, f32) * (fan_in ** -0.5 * scale)

    return {
        'x': jax.random.normal(ks[0], (BATCH, SEQ, D_MODEL), f32),
        'norm_mix': 1.0 + 0.02 * jax.random.normal(ks[1], (DEPTH, D_MODEL), f32),
        'norm_ffn': 1.0 + 0.02 * jax.random.normal(ks[2], (DEPTH, D_MODEL), f32),
        'norm_final': 1.0 + 0.02 * jax.random.normal(ks[3], (D_MODEL,), f32),
        'w_in_attn': w(ks[4], (N_ATTN_LAYERS, D_MODEL, A_PROJ), D_MODEL),
        'w_out_attn': w(ks[5], (N_ATTN_LAYERS, A_WIDTH, D_MODEL), A_WIDTH, out_scale),
        'w_in_ret': w(ks[6], (N_RET_LAYERS, D_MODEL, R_PROJ), D_MODEL),
        'w_out_ret': w(ks[7], (N_RET_LAYERS, R_V_WIDTH, D_MODEL), R_V_WIDTH, out_scale),
        'w_up': w(ks[8], (DEPTH, D_MODEL, 2 * D_FF), D_MODEL),
        'conv_w': w(ks[9], (DEPTH, CONV_WIDTH, 2 * D_FF), CONV_WIDTH),
        'conv_b': 0.01 * jax.random.normal(ks[10], (DEPTH, 2 * D_FF), f32),
        'w_down': w(ks[11], (DEPTH, D_FF, D_MODEL), D_FF, out_scale),
    }


def reference(x, norm_mix, norm_ffn, norm_final, w_in_attn, w_out_attn, w_in_ret, w_out_ret,
              w_up, conv_w, conv_b, w_down):
    S = x.shape[1]
    inv_freq_a = ROPE_THETA ** (-jnp.arange(0, A_HEAD_DIM, 2, dtype=jnp.float32) / A_HEAD_DIM)
    cos_a, sin_a = rotary_tables(S, inv_freq_a)
    inv_freq_r = ROPE_THETA ** (-jnp.linspace(0.0, 1.0, R_QK_DIM // 2, dtype=jnp.float32))
    cos_r, sin_r = rotary_tables(S, inv_freq_r)
    h = x
    for i in range(DEPTH):
        hn = rmsnorm(h, norm_mix[i])
        li = i // N_MIXERS
        if i % N_MIXERS == 0:
            h = h + dilated_attention_mixer(hn, w_in_attn[li], w_out_attn[li], cos_a, sin_a)
        else:
            h = h + retention_mixer(hn, w_in_ret[li], w_out_ret[li], cos_r, sin_r)
        h = h + conv_ffn(rmsnorm(h, norm_ffn[i]), w_up[i], conv_w[i], conv_b[i], w_down[i])
    return rmsnorm(h, norm_final)
```

```python
import functools

import jax
import jax.numpy as jnp
from jax import lax
from jax.experimental import pallas as pl
from jax.experimental.pallas import tpu as pltpu

F32 = jnp.float32
BF16 = jnp.bfloat16

D_MODEL = 2048
EPS = 1e-6
ROPE_THETA = 10000.0

A_HEAD_DIM = 128
A_HEADS = D_MODEL // A_HEAD_DIM
A_WIDTH = A_HEADS * A_HEAD_DIM
A_PATTERNS = ((128, 1), (512, 4), (2048, 16))
A_N_GROUPS = len(A_PATTERNS)
A_BLOCK = 128

R_QK_DIM = 256
R_V_DIM = 512
R_HEADS = D_MODEL // R_QK_DIM
R_QK_WIDTH = R_HEADS * R_QK_DIM
R_V_WIDTH = R_HEADS * R_V_DIM
R_CHUNK = 128

D_FF = 5632
CONV_WIDTH = 3

BF16_ROWS_PER_TILE = 16
LANES = 128
VMEM_LIMIT_BYTES = 56 * 1024 * 1024


def _params(*semantics):
    return pltpu.CompilerParams(dimension_semantics=semantics, vmem_limit_bytes=VMEM_LIMIT_BYTES)


def _rms_scale(h, gain):
    return h * lax.rsqrt(jnp.mean(h * h, axis=-1, keepdims=True) + EPS) * gain


def _rmsnorm_kernel(x_ref, g_ref, o_ref):
    o_ref[...] = _rms_scale(x_ref[...], g_ref[...]).astype(o_ref.dtype)


def _rmsnorm(x, gain, tm=512):
    m, d = x.shape
    return pl.pallas_call(
        _rmsnorm_kernel,
        grid=(m // tm,),
        in_specs=[pl.BlockSpec((tm, d), lambda i: (i, 0)), pl.BlockSpec((1, d), lambda i: (0, 0))],
        out_specs=pl.BlockSpec((tm, d), lambda i: (i, 0)),
        out_shape=jax.ShapeDtypeStruct((m, d), BF16),
        compiler_params=_params("parallel"),
        name="rmsnorm",
    )(x, gain.reshape(1, d))


def _in_proj_kernel(a_ref, w_ref, cos_ref, sin_ref, o_ref, *, head_dim, section, period):
    tn = o_ref.shape[1]
    acc = jnp.dot(a_ref[...], w_ref[...], preferred_element_type=F32)
    sec = (pl.program_id(1) * tn // section) % period
    is_rot = sec < 2

    @pl.when(is_rot)
    def _():
        cos = cos_ref[0]
        sin = sin_ref[0]
        for c in range(tn // head_dim):
            sl = slice(c * head_dim, (c + 1) * head_dim)
            x = acc[:, sl]
            o_ref[:, sl] = (x * cos + pltpu.roll(x, head_dim // 2, 1) * sin).astype(o_ref.dtype)

    @pl.when(jnp.logical_not(is_rot))
    def _():
        o_ref[...] = acc.astype(o_ref.dtype)


def _in_proj(a, w, cos_tab, sin_tab, *, seq, head_dim, period, tm=1024, tn=1024, section=2048):
    m, k = a.shape
    n = w.shape[1]
    seq_tiles = seq // tm

    def tab_map(i, j):
        return (jnp.minimum((j * tn // section) % period, 1), i % seq_tiles, 0)

    return pl.pallas_call(
        functools.partial(_in_proj_kernel, head_dim=head_dim, section=section, period=period),
        grid=(m // tm, n // tn),
        in_specs=[
            pl.BlockSpec((tm, k), lambda i, j: (i, 0)),
            pl.BlockSpec((k, tn), lambda i, j: (0, j)),
            pl.BlockSpec((1, tm, head_dim), tab_map),
            pl.BlockSpec((1, tm, head_dim), tab_map),
        ],
        out_specs=pl.BlockSpec((tm, tn), lambda i, j: (i, j)),
        out_shape=jax.ShapeDtypeStruct((m, n), BF16),
        compiler_params=_params("parallel", "arbitrary"),
        name="in_proj",
    )(a, w, cos_tab, sin_tab)


def _attn_group_kernel(q_ref, kp_ref, kc_ref, vp_ref, vc_ref, o_ref, lse_ref, *, n_back):
    blk = A_BLOCK
    n = pl.program_id(2)
    i = lax.broadcasted_iota(jnp.int32, (blk, 2 * blk), 0)
    j = lax.broadcasted_iota(jnp.int32, (blk, 2 * blk), 1)
    first_key = jnp.maximum(i + blk - n_back, jnp.where(n > 0, 0, blk))
    valid = (j >= first_key) & (j <= i + blk)
    lane = lax.broadcasted_iota(jnp.int32, (blk, LANES), 1)
    lse_tile = jnp.zeros((blk, LANES), F32)
    for h in range(A_HEADS):
        sl = slice(h * A_HEAD_DIM, (h + 1) * A_HEAD_DIM)
        q = q_ref[0, :, sl]
        k = jnp.concatenate([kp_ref[0, :, sl], kc_ref[0, :, sl]], axis=0)
        v = jnp.concatenate([vp_ref[0, :, sl], vc_ref[0, :, sl]], axis=0)
        s = lax.dot_general(q, k, (((1,), (1,)), ((), ())), preferred_element_type=F32)
        s = jnp.where(valid, s, -jnp.inf)
        mx = jnp.max(s, axis=-1, keepdims=True)
        p = jnp.exp(s - mx)
        l = jnp.sum(p, axis=-1, keepdims=True)
        o = jnp.dot(p.astype(BF16), v, preferred_element_type=F32)
        o_ref[0, :, sl] = o / l
        lse_tile = jnp.where(lane == h, mx + jnp.log(l), lse_tile)
    lse_ref[0] = lse_tile


def _attn_group(proj, *, batch, seq, group, window, dilation):
    n_proj = proj.shape[1]
    sub_len = seq // dilation
    nb = sub_len // A_BLOCK
    cols = n_proj // A_WIDTH
    view = proj.reshape(batch, sub_len, dilation * n_proj)
    base = group * 3

    def col(t):
        return lambda b, r, n: (b, n, r * cols + base + t)

    def col_prev(t):
        return lambda b, r, n: (b, jnp.maximum(n - 1, 0), r * cols + base + t)

    blk = (1, A_BLOCK, A_WIDTH)
    o, lse = pl.pallas_call(
        functools.partial(_attn_group_kernel, n_back=window // dilation),
        grid=(batch, dilation, nb),
        in_specs=[
            pl.BlockSpec(blk, col(0)),
            pl.BlockSpec(blk, col_prev(1)),
            pl.BlockSpec(blk, col(1)),
            pl.BlockSpec(blk, col_prev(2)),
            pl.BlockSpec(blk, col(2)),
        ],
        out_specs=[
            pl.BlockSpec(blk, lambda b, r, n: (b, n, r)),
            pl.BlockSpec((1, A_BLOCK, LANES), lambda b, r, n: (b, n, r)),
        ],
        out_shape=[
            jax.ShapeDtypeStruct((batch, sub_len, dilation * A_WIDTH), F32),
            jax.ShapeDtypeStruct((batch, sub_len, dilation * LANES), F32),
        ],
        compiler_params=_params("parallel", "parallel", "arbitrary"),
        name=f"attn_group{group}",
    )(view, view, view, view, view)
    return o.reshape(batch * seq, A_WIDTH), lse.reshape(batch * seq, LANES)


def _attn_combine_kernel(o0_ref, o1_ref, o2_ref, l0_ref, l1_ref, l2_ref, out_ref):
    lses = [l0_ref[...], l1_ref[...], l2_ref[...]]
    mx = jnp.maximum(jnp.maximum(lses[0], lses[1]), lses[2])
    es = [jnp.exp(l - mx) for l in lses]
    tot = es[0] + es[1] + es[2]
    ws = [e / tot for e in es]
    o_refs = (o0_ref, o1_ref, o2_ref)
    for h in range(A_HEADS):
        sl = slice(h * A_HEAD_DIM, (h + 1) * A_HEAD_DIM)
        acc = ws[0][:, h:h + 1] * o_refs[0][:, sl]
        for g in (1, 2):
            acc = acc + ws[g][:, h:h + 1] * o_refs[g][:, sl]
        out_ref[:, sl] = acc.astype(out_ref.dtype)


def _attn_combine(outs, lses, tm=512):
    m = outs[0].shape[0]
    o_spec = pl.BlockSpec((tm, A_WIDTH), lambda i: (i, 0))
    l_spec = pl.BlockSpec((tm, LANES), lambda i: (i, 0))
    return pl.pallas_call(
        _attn_combine_kernel,
        grid=(m // tm,),
        in_specs=[o_spec] * 3 + [l_spec] * 3,
        out_specs=o_spec,
        out_shape=jax.ShapeDtypeStruct((m, A_WIDTH), BF16),
        compiler_params=_params("parallel"),
        name="attn_combine",
    )(*outs, *lses)


def _retention_kernel(q_ref, k_ref, v_ref, g_ref, decay_ref, qdec_ref, kdec_ref, cdec_ref,
                      o_ref, state_ref, *, chunks):
    c_len = R_CHUNK

    @pl.when(pl.program_id(2) == 0)
    def _():
        state_ref[...] = jnp.zeros_like(state_ref)

    decay = decay_ref[0]
    qdec = qdec_ref[0]
    kdec = kdec_ref[0]
    cdec = cdec_ref[0]
    for c in range(chunks):
        rows = slice(c * c_len, (c + 1) * c_len)
        q = q_ref[rows, :]
        k = k_ref[rows, :]
        v = v_ref[rows, :]
        scores = lax.dot_general(q, k, (((1,), (1,)), ((), ())), preferred_element_type=F32) * decay
        y = jnp.dot(scores.astype(BF16), v, preferred_element_type=F32)
        state = state_ref[...]
        y = y + jnp.dot(q, state.astype(BF16), preferred_element_type=F32) * qdec
        k_dec = (k.astype(F32) * kdec).astype(BF16)
        state_ref[...] = state * cdec + lax.dot_general(
            k_dec, v, (((0,), (0,)), ((), ())), preferred_element_type=F32)
        mu = jnp.mean(y, axis=-1, keepdims=True)
        yc = y - mu
        var = jnp.mean(yc * yc, axis=-1, keepdims=True)
        yn = yc * lax.rsqrt(var + EPS)
        g = g_ref[rows, :].astype(F32)
        o_ref[rows, :] = (g / (1.0 + jnp.exp(-g)) * yn).astype(o_ref.dtype)


def _retention_tables():
    heads = R_HEADS
    c_len = R_CHUNK
    log_gamma = jnp.log1p(-jnp.exp2(-5.0 - jnp.arange(heads, dtype=F32)))
    idx = jnp.arange(c_len, dtype=F32)
    rel = idx[:, None] - idx[None, :]
    decay = jnp.where(rel >= 0, jnp.exp(log_gamma[:, None, None] * jnp.maximum(rel, 0.0)), 0.0)
    q_decay = jnp.exp(log_gamma[None, :] * (idx[:, None] + 1.0))
    k_decay = jnp.exp(log_gamma[None, :] * (c_len - 1.0 - idx[:, None]))
    chunk_decay = jnp.exp(log_gamma * c_len)
    qdec = jnp.broadcast_to(q_decay.T[:, :, None], (heads, c_len, R_V_DIM))
    kdec = jnp.broadcast_to(k_decay.T[:, :, None], (heads, c_len, R_QK_DIM))
    cdec = jnp.broadcast_to(chunk_decay[:, None, None], (heads, 1, R_V_DIM))
    return decay, qdec, kdec, cdec


def _retention(proj, *, batch, seq, chunks=8):
    m = proj.shape[0]
    rows = chunks * R_CHUNK
    steps = seq // rows
    decay, qdec, kdec, cdec = _retention_tables()
    k_off = R_QK_WIDTH // R_QK_DIM
    v_off = 2 * R_QK_WIDTH // R_V_DIM
    g_off = (2 * R_QK_WIDTH + R_V_WIDTH) // R_V_DIM

    def tok(off):
        return lambda b, h, t: (b * steps + t, off + h)

    def per_head(b, h, t):
        return (h, 0, 0)

    return pl.pallas_call(
        functools.partial(_retention_kernel, chunks=chunks),
        grid=(batch, R_HEADS, steps),
        in_specs=[
            pl.BlockSpec((rows, R_QK_DIM), tok(0)),
            pl.BlockSpec((rows, R_QK_DIM), tok(k_off)),
            pl.BlockSpec((rows, R_V_DIM), tok(v_off)),
            pl.BlockSpec((rows, R_V_DIM), tok(g_off)),
            pl.BlockSpec((1, R_CHUNK, R_CHUNK), per_head),
            pl.BlockSpec((1, R_CHUNK, R_V_DIM), per_head),
            pl.BlockSpec((1, R_CHUNK, R_QK_DIM), per_head),
            pl.BlockSpec((1, 1, R_V_DIM), per_head),
        ],
        out_specs=pl.BlockSpec((rows, R_V_DIM), tok(0)),
        out_shape=jax.ShapeDtypeStruct((m, R_V_WIDTH), BF16),
        scratch_shapes=[pltpu.VMEM((R_QK_DIM, R_V_DIM), F32)],
        compiler_params=_params("parallel", "parallel", "arbitrary"),
        name="retention",
    )(proj, proj, proj, proj, decay, qdec, kdec, cdec)


def _ffn_up_kernel(a_ref, halo_ref, wg_ref, wu_ref, cwg_ref, cwu_ref, cbg_ref, cbu_ref, o_ref,
                   *, tiles_per_seq):
    tm = a_ref.shape[0]
    pad = halo_ref.shape[0]
    at_seq_start = pl.program_id(0) % tiles_per_seq == 0
    halo = halo_ref[...]
    halo = jnp.where(at_seq_start, jnp.zeros_like(halo), halo)
    a = jnp.concatenate([halo, a_ref[...]], axis=0)

    def conv(w_ref, cw_ref, cb_ref):
        u = jnp.dot(a, w_ref[...], preferred_element_type=F32)
        cw = cw_ref[...]
        y = cb_ref[...] + cw[CONV_WIDTH - 1:CONV_WIDTH] * u[pad:pad + tm]
        for back in range(1, CONV_WIDTH):
            tap = CONV_WIDTH - 1 - back
            y = y + cw[tap:tap + 1] * u[pad - back:pad - back + tm]
        return y

    gate = conv(wg_ref, cwg_ref, cbg_ref)
    up = conv(wu_ref, cwu_ref, cbu_ref)
    o_ref[...] = (gate / (1.0 + jnp.exp(-gate)) * up).astype(o_ref.dtype)


def _ffn_up(a, w_up, conv_w, conv_b, *, seq, tm=512, tn=512):
    m, k = a.shape
    pad = BF16_ROWS_PER_TILE
    n_tiles = D_FF // tn
    halo_blocks = tm // pad

    def halo_map(i, j):
        return (jnp.maximum(i * halo_blocks - 1, 0), 0)

    conv_b = conv_b.reshape(1, 2 * D_FF)
    return pl.pallas_call(
        functools.partial(_ffn_up_kernel, tiles_per_seq=seq // tm),
        grid=(m // tm, n_tiles),
        in_specs=[
            pl.BlockSpec((tm, k), lambda i, j: (i, 0)),
            pl.BlockSpec((pad, k), halo_map),
            pl.BlockSpec((k, tn), lambda i, j: (0, j)),
            pl.BlockSpec((k, tn), lambda i, j: (0, n_tiles + j)),
            pl.BlockSpec((CONV_WIDTH, tn), lambda i, j: (0, j)),
            pl.BlockSpec((CONV_WIDTH, tn), lambda i, j: (0, n_tiles + j)),
            pl.BlockSpec((1, tn), lambda i, j: (0, j)),
            pl.BlockSpec((1, tn), lambda i, j: (0, n_tiles + j)),
        ],
        out_specs=pl.BlockSpec((tm, tn), lambda i, j: (i, j)),
        out_shape=jax.ShapeDtypeStruct((m, D_FF), BF16),
        compiler_params=_params("parallel", "arbitrary"),
        name="ffn_up",
    )(a, a, w_up, w_up, conv_w, conv_w, conv_b, conv_b)


def _out_proj_kernel(a_ref, w_ref, h_ref, g_ref, *refs, final):
    if final:
        hn_ref, acc_ref = refs
    else:
        ho_ref, hn_ref, acc_ref = refs
    kk = pl.program_id(1)

    @pl.when(kk == 0)
    def _():
        acc_ref[...] = h_ref[...]

    acc_ref[...] += jnp.dot(a_ref[...], w_ref[...], preferred_element_type=F32)

    @pl.when(kk == pl.num_programs(1) - 1)
    def _():
        h_new = acc_ref[...]
        if not final:
            ho_ref[...] = h_new
        hn_ref[...] = _rms_scale(h_new, g_ref[...]).astype(hn_ref.dtype)


def _out_proj(a, w, h, gain, *, final=False, tm=512, tk=512):
    m, k = a.shape
    d = w.shape[1]
    row = pl.BlockSpec((tm, d), lambda i, kk: (i, 0))
    if final:
        out_specs = row
        out_shape = jax.ShapeDtypeStruct((m, d), F32)
    else:
        out_specs = [row, row]
        out_shape = [jax.ShapeDtypeStruct((m, d), F32), jax.ShapeDtypeStruct((m, d), BF16)]
    return pl.pallas_call(
        functools.partial(_out_proj_kernel, final=final),
        grid=(m // tm, k // tk),
        in_specs=[
            pl.BlockSpec((tm, tk), lambda i, kk: (i, kk)),
            pl.BlockSpec((tk, d), lambda i, kk: (kk, 0)),
            row,
            pl.BlockSpec((1, d), lambda i, kk: (0, 0)),
        ],
        out_specs=out_specs,
        out_shape=out_shape,
        scratch_shapes=[pltpu.VMEM((tm, d), F32)],
        compiler_params=_params("parallel", "arbitrary"),
        name="out_proj",
    )(a, w, h, gain.reshape(1, d))


def _rotary_tables(seq, inv_freq, q_scale, k_scale):
    ang = jnp.arange(seq, dtype=F32)[:, None] * inv_freq[None, :]
    cos, sin = jnp.cos(ang), jnp.sin(ang)
    cos_full = jnp.concatenate([cos, cos], axis=-1)
    sin_signed = jnp.concatenate([-sin, sin], axis=-1)
    scales = jnp.array([q_scale, k_scale], F32)[:, None, None]
    return cos_full[None] * scales, sin_signed[None] * scales


def kernel(x, norm_mix, norm_ffn, norm_final, w_in_attn, w_out_attn, w_in_ret, w_out_ret,
           w_up, conv_w, conv_b, w_down):
    batch, seq, d = x.shape
    depth = norm_mix.shape[0]
    inv_freq_a = ROPE_THETA ** (-jnp.arange(0, A_HEAD_DIM, 2, dtype=F32) / A_HEAD_DIM)
    cos_a, sin_a = _rotary_tables(seq, inv_freq_a, A_HEAD_DIM ** -0.5, 1.0)
    inv_freq_r = ROPE_THETA ** (-jnp.linspace(0.0, 1.0, R_QK_DIM // 2, dtype=F32))
    cos_r, sin_r = _rotary_tables(seq, inv_freq_r, 1.0, R_QK_DIM ** -0.5)

    h = x.reshape(batch * seq, d)
    hn = _rmsnorm(h, norm_mix[0])
    out = None
    for i in range(depth):
        li = i // 2
        if i % 2 == 0:
            proj = _in_proj(hn, w_in_attn[li].astype(BF16), cos_a, sin_a, seq=seq,
                            head_dim=A_HEAD_DIM, period=3)
            outs, lses = [], []
            for g, (window, dilation) in enumerate(A_PATTERNS):
                o, lse = _attn_group(proj, batch=batch, seq=seq, group=g, window=window,
                                     dilation=dilation)
                outs.append(o)
                lses.append(lse)
            mixed = _attn_combine(outs, lses)
            h, hn = _out_proj(mixed, w_out_attn[li].astype(BF16), h, norm_ffn[i])
        else:
            proj = _in_proj(hn, w_in_ret[li].astype(BF16), cos_r, sin_r, seq=seq,
                            head_dim=R_QK_DIM, period=6)
            mixed = _retention(proj, batch=batch, seq=seq)
            h, hn = _out_proj(mixed, w_out_ret[li].astype(BF16), h, norm_ffn[i])
        act = _ffn_up(hn, w_up[i].astype(BF16), conv_w[i], conv_b[i], seq=seq)
        if i + 1 < depth:
            h, hn = _out_proj(act, w_down[i].astype(BF16), h, norm_mix[i + 1])
        else:
            out = _out_proj(act, w_down[i].astype(BF16), h, norm_final, final=True)
    return out.reshape(batch, seq, d)
```

```python
import functools

import jax
import jax.numpy as jnp
from jax import lax
from jax.experimental import pallas as pl
from jax.experimental.pallas import tpu as pltpu

F32 = jnp.float32
BF16 = jnp.bfloat16

D_MODEL = 2048
EPS = 1e-6
ROPE_THETA = 10000.0

A_HEAD_DIM = 128
A_HEADS = D_MODEL // A_HEAD_DIM
A_WIDTH = A_HEADS * A_HEAD_DIM
A_PATTERNS = ((128, 1), (512, 4), (2048, 16))
A_N_GROUPS = len(A_PATTERNS)
A_BLOCK = 128

R_QK_DIM = 256
R_V_DIM = 512
R_HEADS = D_MODEL // R_QK_DIM
R_QK_WIDTH = R_HEADS * R_QK_DIM
R_V_WIDTH = R_HEADS * R_V_DIM
R_CHUNK = 128

D_FF = 5632
CONV_WIDTH = 3

BF16_ROWS_PER_TILE = 16
LANES = 128
VMEM_LIMIT_BYTES = 56 * 1024 * 1024


def _params(*semantics):
    return pltpu.CompilerParams(dimension_semantics=semantics, vmem_limit_bytes=VMEM_LIMIT_BYTES)


def _rms_scale(h, gain):
    return h * lax.rsqrt(jnp.mean(h * h, axis=-1, keepdims=True) + EPS) * gain


def _rmsnorm_kernel(x_ref, g_ref, o_ref):
    o_ref[...] = _rms_scale(x_ref[...], g_ref[...]).astype(o_ref.dtype)


def _rmsnorm(x, gain, tm=512):
    m, d = x.shape
    return pl.pallas_call(
        _rmsnorm_kernel,
        grid=(m // tm,),
        in_specs=[pl.BlockSpec((tm, d), lambda i: (i, 0)), pl.BlockSpec((1, d), lambda i: (0, 0))],
        out_specs=pl.BlockSpec((tm, d), lambda i: (i, 0)),
        out_shape=jax.ShapeDtypeStruct((m, d), BF16),
        compiler_params=_params("parallel"),
        name="rmsnorm",
    )(x, gain.reshape(1, d))


def _rotate(x, cos, sin, head_dim):
    return x * cos + pltpu.roll(x, head_dim // 2, 1) * sin


def _in_proj_kernel(a_ref, w_ref, cos_ref, sin_ref, o_ref, *scratch, head_dim, rot_tiles,
                    dilation):
    tn = o_ref.shape[-1]
    acc = jnp.dot(a_ref[...], w_ref[...], preferred_element_type=F32)
    is_rot = pl.program_id(1) < rot_tiles

    if dilation is None:
        @pl.when(is_rot)
        def _():
            cos = cos_ref[0]
            sin = sin_ref[0]
            for c in range(tn // head_dim):
                sl = slice(c * head_dim, (c + 1) * head_dim)
                o_ref[:, sl] = _rotate(acc[:, sl], cos, sin, head_dim).astype(o_ref.dtype)

        @pl.when(jnp.logical_not(is_rot))
        def _():
            o_ref[...] = acc.astype(o_ref.dtype)
        return

    assert head_dim == LANES
    (acc_ref,) = scratch
    n_slabs = tn // LANES
    sub_rows = a_ref.shape[0] // dilation
    for c in range(n_slabs):
        acc_ref[c] = acc[:, c * LANES:(c + 1) * LANES]

    def residue_rows(r):
        return pl.ds(r, sub_rows, stride=dilation) if dilation > 1 else slice(None)

    @pl.when(is_rot)
    def _():
        for r in range(dilation):
            cos = cos_ref[0, residue_rows(r), :]
            sin = sin_ref[0, residue_rows(r), :]
            for c in range(n_slabs):
                x = acc_ref[c, residue_rows(r), :]
                o_ref[0, r, :, c * LANES:(c + 1) * LANES] = _rotate(x, cos, sin, head_dim).astype(
                    o_ref.dtype)

    @pl.when(jnp.logical_not(is_rot))
    def _():
        for r in range(dilation):
            for c in range(n_slabs):
                o_ref[0, r, :, c * LANES:(c + 1) * LANES] = acc_ref[c, residue_rows(r), :].astype(
                    o_ref.dtype)


def _in_proj(a, w, cos_tab, sin_tab, *, batch, seq, head_dim, n_out, col_offset=0, dilation=None,
             tm=1024, tn=1024, section=2048):
    m, k = a.shape
    seq_tiles = seq // tm
    col_tile0 = col_offset // tn
    rot_tiles = 2 * section // tn

    def tab_map(i, j):
        return (jnp.minimum(j * tn // section, 1), i % seq_tiles, 0)

    if dilation is None:
        out_spec = pl.BlockSpec((tm, tn), lambda i, j: (i, j))
        out_shape = jax.ShapeDtypeStruct((m, n_out), BF16)
        scratch = []
    else:
        out_spec = pl.BlockSpec((1, dilation, tm // dilation, tn),
                                lambda i, j: (i // seq_tiles, 0, i % seq_tiles, j))
        out_shape = jax.ShapeDtypeStruct((batch, dilation, seq // dilation, n_out), BF16)
        scratch = [pltpu.VMEM((tn // LANES, tm, LANES), F32)]
    return pl.pallas_call(
        functools.partial(_in_proj_kernel, head_dim=head_dim, rot_tiles=rot_tiles,
                          dilation=dilation),
        grid=(m // tm, n_out // tn),
        in_specs=[
            pl.BlockSpec((tm, k), lambda i, j: (i, 0)),
            pl.BlockSpec((k, tn), lambda i, j: (0, col_tile0 + j)),
            pl.BlockSpec((1, tm, head_dim), tab_map),
            pl.BlockSpec((1, tm, head_dim), tab_map),
        ],
        out_specs=out_spec,
        out_shape=out_shape,
        scratch_shapes=scratch,
        compiler_params=_params("parallel", "arbitrary"),
        name="in_proj" if dilation is None else f"in_proj_d{dilation}",
    )(a, w, cos_tab, sin_tab)


A_SPAN = max(d for _, d in A_PATTERNS) * A_BLOCK
A_COMBINE_ROWS = 256


def _attn_kernel(*refs, heads):
    in_refs, out_ref, o_nat, lse_nat = refs[:-3], refs[-3], refs[-2], refs[-1]
    blk = A_BLOCK
    span = pl.program_id(1)
    qi = lax.broadcasted_iota(jnp.int32, (blk, 2 * blk), 0)
    kj = lax.broadcasted_iota(jnp.int32, (blk, 2 * blk), 1)

    for g, (window, dilation) in enumerate(A_PATTERNS):
        q_ref, kp_ref, kc_ref, vp_ref, vc_ref = in_refs[5 * g:5 * g + 5]
        n_back = window // dilation
        nq = A_SPAN // (dilation * blk)
        band = (kj >= qi + blk - n_back) & (kj <= qi + blk)
        band_first = (kj >= jnp.maximum(qi + blk - n_back, jnp.where(span > 0, 0, blk))) & (
            kj <= qi + blk)
        for r in range(dilation):
            for bq in range(nq):
                if dilation > 1:
                    rows = pl.ds(bq * blk * dilation + r, blk, stride=dilation)
                else:
                    rows = slice(bq * blk, (bq + 1) * blk)
                for h in range(heads):
                    hs = slice(h * A_HEAD_DIM, (h + 1) * A_HEAD_DIM)
                    q = q_ref[0, r, bq * blk:(bq + 1) * blk, hs]
                    if bq == 0:
                        k = jnp.concatenate([kp_ref[0, r, :, hs], kc_ref[0, r, 0:blk, hs]], axis=0)
                        v = jnp.concatenate([vp_ref[0, r, :, hs], vc_ref[0, r, 0:blk, hs]], axis=0)
                        valid = band_first
                    else:
                        k = kc_ref[0, r, (bq - 1) * blk:(bq + 1) * blk, hs]
                        v = vc_ref[0, r, (bq - 1) * blk:(bq + 1) * blk, hs]
                        valid = band
                    s = lax.dot_general(q, k, (((1,), (1,)), ((), ())),
                                        preferred_element_type=F32)
                    s = jnp.where(valid, s, -jnp.inf)
                    mx = jnp.max(s, axis=-1, keepdims=True)
                    p = jnp.exp(s - mx)
                    l = jnp.sum(p, axis=-1, keepdims=True)
                    o = jnp.dot(p.astype(BF16), v, preferred_element_type=F32)
                    o_nat[g * heads + h, rows, :] = o / l
                    lse_nat[g * heads + h, rows, :] = jnp.broadcast_to(mx + jnp.log(l), (blk, LANES))

    n_groups = len(A_PATTERNS)
    for h in range(heads):
        def combine(c, carry, h=h):
            rs = pl.ds(pl.multiple_of(c * A_COMBINE_ROWS, A_COMBINE_ROWS), A_COMBINE_ROWS)
            lses = [lse_nat[g * heads + h, rs, :] for g in range(n_groups)]
            mx = functools.reduce(jnp.maximum, lses)
            es = [jnp.exp(l - mx) for l in lses]
            num = sum(e * o_nat[g * heads + h, rs, :] for g, e in enumerate(es))
            out_ref[rs, h * A_HEAD_DIM:(h + 1) * A_HEAD_DIM] = (num / sum(es)).astype(out_ref.dtype)
            return carry
        lax.fori_loop(0, A_SPAN // A_COMBINE_ROWS, combine, 0)


def _attention(projs, *, batch, seq, heads=2):
    width = heads * A_HEAD_DIM
    sec_blocks = A_WIDTH // width
    n_spans = seq // A_SPAN
    in_specs = []
    for (window, dilation) in A_PATTERNS:
        sub_rows = A_SPAN // dilation
        nq = sub_rows // A_BLOCK

        def cur(t, dilation=dilation, sub_rows=sub_rows):
            return pl.BlockSpec((1, dilation, sub_rows, width),
                                lambda b, s, hh: (b, 0, s, t * sec_blocks + hh))

        def prev(t, dilation=dilation, nq=nq):
            return pl.BlockSpec((1, dilation, A_BLOCK, width),
                                lambda b, s, hh: (b, 0, jnp.maximum(s * nq - 1, 0),
                                                  t * sec_blocks + hh))

        in_specs += [cur(0), prev(1), cur(1), prev(2), cur(2)]
    n_slabs = len(A_PATTERNS) * heads
    return pl.pallas_call(
        functools.partial(_attn_kernel, heads=heads),
        grid=(batch, n_spans, sec_blocks),
        in_specs=in_specs,
        out_specs=pl.BlockSpec((A_SPAN, width), lambda b, s, hh: (b * n_spans + s, hh)),
        out_shape=jax.ShapeDtypeStruct((batch * seq, A_WIDTH), BF16),
        scratch_shapes=[pltpu.VMEM((n_slabs, A_SPAN, LANES), F32),
                        pltpu.VMEM((n_slabs, A_SPAN, LANES), F32)],
        compiler_params=_params("parallel", "parallel", "parallel"),
        name="attention",
    )(*[p for p in projs for _ in range(5)])


def _retention_kernel(q_ref, k_ref, v_ref, g_ref, decay_ref, qdec_ref, kdec_ref, cdec_ref,
                      o_ref, state_ref, *, chunks):
    c_len = R_CHUNK

    @pl.when(pl.program_id(2) == 0)
    def _():
        state_ref[...] = jnp.zeros_like(state_ref)

    decay = decay_ref[0]
    qdec = qdec_ref[0]
    kdec = kdec_ref[0]
    cdec = cdec_ref[0]
    for c in range(chunks):
        rows = slice(c * c_len, (c + 1) * c_len)
        q = q_ref[rows, :]
        k = k_ref[rows, :]
        v = v_ref[rows, :]
        scores = lax.dot_general(q, k, (((1,), (1,)), ((), ())), preferred_element_type=F32) * decay
        y = jnp.dot(scores.astype(BF16), v, preferred_element_type=F32)
        state = state_ref[...]
        y = y + jnp.dot(q, state.astype(BF16), preferred_element_type=F32) * qdec
        k_dec = (k.astype(F32) * kdec).astype(BF16)
        state_ref[...] = state * cdec + lax.dot_general(
            k_dec, v, (((0,), (0,)), ((), ())), preferred_element_type=F32)
        mu = jnp.mean(y, axis=-1, keepdims=True)
        yc = y - mu
        var = jnp.mean(yc * yc, axis=-1, keepdims=True)
        yn = yc * lax.rsqrt(var + EPS)
        g = g_ref[rows, :].astype(F32)
        o_ref[rows, :] = (g / (1.0 + jnp.exp(-g)) * yn).astype(o_ref.dtype)


def _retention_tables():
    heads = R_HEADS
    c_len = R_CHUNK
    log_gamma = jnp.log1p(-jnp.exp2(-5.0 - jnp.arange(heads, dtype=F32)))
    idx = jnp.arange(c_len, dtype=F32)
    rel = idx[:, None] - idx[None, :]
    decay = jnp.where(rel >= 0, jnp.exp(log_gamma[:, None, None] * jnp.maximum(rel, 0.0)), 0.0)
    q_decay = jnp.exp(log_gamma[None, :] * (idx[:, None] + 1.0))
    k_decay = jnp.exp(log_gamma[None, :] * (c_len - 1.0 - idx[:, None]))
    chunk_decay = jnp.exp(log_gamma * c_len)
    qdec = jnp.broadcast_to(q_decay.T[:, :, None], (heads, c_len, R_V_DIM))
    kdec = jnp.broadcast_to(k_decay.T[:, :, None], (heads, c_len, R_QK_DIM))
    cdec = jnp.broadcast_to(chunk_decay[:, None, None], (heads, 1, R_V_DIM))
    return decay, qdec, kdec, cdec


def _retention(proj, *, batch, seq, chunks=8):
    m = proj.shape[0]
    rows = chunks * R_CHUNK
    steps = seq // rows
    decay, qdec, kdec, cdec = _retention_tables()
    k_off = R_QK_WIDTH // R_QK_DIM
    v_off = 2 * R_QK_WIDTH // R_V_DIM
    g_off = (2 * R_QK_WIDTH + R_V_WIDTH) // R_V_DIM

    def tok(off):
        return lambda b, h, t: (b * steps + t, off + h)

    def per_head(b, h, t):
        return (h, 0, 0)

    return pl.pallas_call(
        functools.partial(_retention_kernel, chunks=chunks),
        grid=(batch, R_HEADS, steps),
        in_specs=[
            pl.BlockSpec((rows, R_QK_DIM), tok(0)),
            pl.BlockSpec((rows, R_QK_DIM), tok(k_off)),
            pl.BlockSpec((rows, R_V_DIM), tok(v_off)),
            pl.BlockSpec((rows, R_V_DIM), tok(g_off)),
            pl.BlockSpec((1, R_CHUNK, R_CHUNK), per_head),
            pl.BlockSpec((1, R_CHUNK, R_V_DIM), per_head),
            pl.BlockSpec((1, R_CHUNK, R_QK_DIM), per_head),
            pl.BlockSpec((1, 1, R_V_DIM), per_head),
        ],
        out_specs=pl.BlockSpec((rows, R_V_DIM), tok(0)),
        out_shape=jax.ShapeDtypeStruct((m, R_V_WIDTH), BF16),
        scratch_shapes=[pltpu.VMEM((R_QK_DIM, R_V_DIM), F32)],
        compiler_params=_params("parallel", "parallel", "arbitrary"),
        name="retention",
    )(proj, proj, proj, proj, decay, qdec, kdec, cdec)


def _ffn_up_kernel(a_ref, halo_ref, wg_ref, wu_ref, cwg_ref, cwu_ref, cbg_ref, cbu_ref, o_ref,
                   *, tiles_per_seq):
    tm = a_ref.shape[0]
    pad = halo_ref.shape[0]
    at_seq_start = pl.program_id(0) % tiles_per_seq == 0
    halo = halo_ref[...]
    halo = jnp.where(at_seq_start, jnp.zeros_like(halo), halo)
    a = jnp.concatenate([halo, a_ref[...]], axis=0)

    def conv(w_ref, cw_ref, cb_ref):
        u = jnp.dot(a, w_ref[...], preferred_element_type=F32)
        cw = cw_ref[...]
        y = cb_ref[...] + cw[CONV_WIDTH - 1:CONV_WIDTH] * u[pad:pad + tm]
        for back in range(1, CONV_WIDTH):
            tap = CONV_WIDTH - 1 - back
            y = y + cw[tap:tap + 1] * u[pad - back:pad - back + tm]
        return y

    gate = conv(wg_ref, cwg_ref, cbg_ref)
    up = conv(wu_ref, cwu_ref, cbu_ref)
    o_ref[...] = (gate / (1.0 + jnp.exp(-gate)) * up).astype(o_ref.dtype)


def _ffn_up(a, w_up, conv_w, conv_b, *, seq, tm=512, tn=512):
    m, k = a.shape
    pad = BF16_ROWS_PER_TILE
    n_tiles = D_FF // tn
    halo_blocks = tm // pad

    def halo_map(i, j):
        return (jnp.maximum(i * halo_blocks - 1, 0), 0)

    conv_b = conv_b.reshape(1, 2 * D_FF)
    return pl.pallas_call(
        functools.partial(_ffn_up_kernel, tiles_per_seq=seq // tm),
        grid=(m // tm, n_tiles),
        in_specs=[
            pl.BlockSpec((tm, k), lambda i, j: (i, 0)),
            pl.BlockSpec((pad, k), halo_map),
            pl.BlockSpec((k, tn), lambda i, j: (0, j)),
            pl.BlockSpec((k, tn), lambda i, j: (0, n_tiles + j)),
            pl.BlockSpec((CONV_WIDTH, tn), lambda i, j: (0, j)),
            pl.BlockSpec((CONV_WIDTH, tn), lambda i, j: (0, n_tiles + j)),
            pl.BlockSpec((1, tn), lambda i, j: (0, j)),
            pl.BlockSpec((1, tn), lambda i, j: (0, n_tiles + j)),
        ],
        out_specs=pl.BlockSpec((tm, tn), lambda i, j: (i, j)),
        out_shape=jax.ShapeDtypeStruct((m, D_FF), BF16),
        compiler_params=_params("parallel", "arbitrary"),
        name="ffn_up",
    )(a, a, w_up, w_up, conv_w, conv_w, conv_b, conv_b)


def _out_proj_kernel(a_ref, w_ref, h_ref, g_ref, *refs, final):
    if final:
        hn_ref, acc_ref = refs
    else:
        ho_ref, hn_ref, acc_ref = refs
    kk = pl.program_id(1)

    @pl.when(kk == 0)
    def _():
        acc_ref[...] = h_ref[...]

    acc_ref[...] += jnp.dot(a_ref[...], w_ref[...], preferred_element_type=F32)

    @pl.when(kk == pl.num_programs(1) - 1)
    def _():
        h_new = acc_ref[...]
        if not final:
            ho_ref[...] = h_new
        hn_ref[...] = _rms_scale(h_new, g_ref[...]).astype(hn_ref.dtype)


def _out_proj(a, w, h, gain, *, final=False, tm=512, tk=512):
    m, k = a.shape
    d = w.shape[1]
    row = pl.BlockSpec((tm, d), lambda i, kk: (i, 0))
    if final:
        out_specs = row
        out_shape = jax.ShapeDtypeStruct((m, d), F32)
    else:
        out_specs = [row, row]
        out_shape = [jax.ShapeDtypeStruct((m, d), F32), jax.ShapeDtypeStruct((m, d), BF16)]
    return pl.pallas_call(
        functools.partial(_out_proj_kernel, final=final),
        grid=(m // tm, k // tk),
        in_specs=[
            pl.BlockSpec((tm, tk), lambda i, kk: (i, kk)),
            pl.BlockSpec((tk, d), lambda i, kk: (kk, 0)),
            row,
            pl.BlockSpec((1, d), lambda i, kk: (0, 0)),
        ],
        out_specs=out_specs,
        out_shape=out_shape,
        scratch_shapes=[pltpu.VMEM((tm, d), F32)],
        compiler_params=_params("parallel", "arbitrary"),
        name="out_proj",
    )(a, w, h, gain.reshape(1, d))


def _rotary_tables(seq, inv_freq, q_scale, k_scale):
    ang = jnp.arange(seq, dtype=F32)[:, None] * inv_freq[None, :]
    cos, sin = jnp.cos(ang), jnp.sin(ang)
    cos_full = jnp.concatenate([cos, cos], axis=-1)
    sin_signed = jnp.concatenate([-sin, sin], axis=-1)
    scales = jnp.array([q_scale, k_scale], F32)[:, None, None]
    return cos_full[None] * scales, sin_signed[None] * scales


def kernel(x, norm_mix, norm_ffn, norm_final, w_in_attn, w_out_attn, w_in_ret, w_out_ret,
           w_up, conv_w, conv_b, w_down):
    batch, seq, d = x.shape
    depth = norm_mix.shape[0]
    inv_freq_a = ROPE_THETA ** (-jnp.arange(0, A_HEAD_DIM, 2, dtype=F32) / A_HEAD_DIM)
    cos_a, sin_a = _rotary_tables(seq, inv_freq_a, A_HEAD_DIM ** -0.5, 1.0)
    inv_freq_r = ROPE_THETA ** (-jnp.linspace(0.0, 1.0, R_QK_DIM // 2, dtype=F32))
    cos_r, sin_r = _rotary_tables(seq, inv_freq_r, 1.0, R_QK_DIM ** -0.5)

    h = x.reshape(batch * seq, d)
    hn = _rmsnorm(h, norm_mix[0])
    out = None
    for i in range(depth):
        li = i // 2
        if i % 2 == 0:
            w_in = w_in_attn[li].astype(BF16)
            group_cols = w_in.shape[1] // A_N_GROUPS
            projs = [
                _in_proj(hn, w_in, cos_a, sin_a, batch=batch, seq=seq, head_dim=A_HEAD_DIM,
                         n_out=group_cols, col_offset=g * group_cols, dilation=dilation)
                for g, (_, dilation) in enumerate(A_PATTERNS)
            ]
            mixed = _attention(projs, batch=batch, seq=seq)
            h, hn = _out_proj(mixed, w_out_attn[li].astype(BF16), h, norm_ffn[i])
        else:
            w_in = w_in_ret[li].astype(BF16)
            proj = _in_proj(hn, w_in, cos_r, sin_r, batch=batch, seq=seq, head_dim=R_QK_DIM,
                            n_out=w_in.shape[1])
            mixed = _retention(proj, batch=batch, seq=seq)
            h, hn = _out_proj(mixed, w_out_ret[li].astype(BF16), h, norm_ffn[i])
        act = _ffn_up(hn, w_up[i].astype(BF16), conv_w[i], conv_b[i], seq=seq)
        if i + 1 < depth:
            h, hn = _out_proj(act, w_down[i].astype(BF16), h, norm_mix[i + 1])
        else:
            out = _out_proj(act, w_down[i].astype(BF16), h, norm_final, final=True)
    return out.reshape(batch, seq, d)
```

```python
import functools

import jax
import jax.numpy as jnp
from jax import lax
from jax.experimental import pallas as pl
from jax.experimental.pallas import tpu as pltpu

F32 = jnp.float32
BF16 = jnp.bfloat16

D_MODEL = 2048
EPS = 1e-6
ROPE_THETA = 10000.0

A_HEAD_DIM = 128
A_HEADS = D_MODEL // A_HEAD_DIM
A_WIDTH = A_HEADS * A_HEAD_DIM
A_PATTERNS = ((128, 1), (512, 4), (2048, 16))
A_N_GROUPS = len(A_PATTERNS)
A_BLOCK = 128

R_QK_DIM = 256
R_V_DIM = 512
R_HEADS = D_MODEL // R_QK_DIM
R_QK_WIDTH = R_HEADS * R_QK_DIM
R_V_WIDTH = R_HEADS * R_V_DIM
R_CHUNK = 128

D_FF = 5632
CONV_WIDTH = 3

BF16_ROWS_PER_TILE = 16
LANES = 128
MXU_COLS = 256
NORM_ROWS = 256
VMEM_LIMIT_BYTES = 56 * 1024 * 1024


def _params(*semantics):
    return pltpu.CompilerParams(dimension_semantics=semantics, vmem_limit_bytes=VMEM_LIMIT_BYTES)


def _rms_scale(h, gain):
    return h * lax.rsqrt(jnp.mean(h * h, axis=-1, keepdims=True) + EPS) * gain


def _rms_rows_to(dst_ref, dst_row0, src_ref, gain, rows_per_pass=NORM_ROWS):
    n_rows = src_ref.shape[0]
    for r0 in range(0, n_rows, rows_per_pass):
        r1 = min(r0 + rows_per_pass, n_rows)
        dst_ref[dst_row0 + r0:dst_row0 + r1, :] = _rms_scale(src_ref[r0:r1, :], gain).astype(
            dst_ref.dtype)


def _rmsnorm_kernel(x_ref, g_ref, o_ref):
    o_ref[...] = _rms_scale(x_ref[...], g_ref[...]).astype(o_ref.dtype)


def _rmsnorm(x, gain, tm=512):
    m, d = x.shape
    return pl.pallas_call(
        _rmsnorm_kernel,
        grid=(m // tm,),
        in_specs=[pl.BlockSpec((tm, d), lambda i: (i, 0)), pl.BlockSpec((1, d), lambda i: (0, 0))],
        out_specs=pl.BlockSpec((tm, d), lambda i: (i, 0)),
        out_shape=jax.ShapeDtypeStruct((m, d), x.dtype),
        compiler_params=_params("parallel"),
        name="rmsnorm",
    )(x, gain.reshape(1, d))


def _rotate(x, cos, sin, head_dim):
    return x * cos + pltpu.roll(x, head_dim // 2, 1) * sin


def _in_proj_kernel(h_ref, g_ref, w_ref, cos_ref, sin_ref, o_ref, hn_ref, acc_ref, *scratch,
                    head_dim, rot_tiles, dilation):
    j = pl.program_id(1)

    @pl.when(j == 0)
    def _():
        _rms_rows_to(hn_ref, 0, h_ref, g_ref[...])

    tm = hn_ref.shape[0]
    tn = o_ref.shape[-1]
    chunk = acc_ref.shape[-1]
    is_rot = j < rot_tiles
    if dilation is not None:
        assert head_dim == LANES
        (slab_ref,) = scratch
        sub_rows = tm // dilation

    for c in range(tn // chunk):
        acc = acc_ref.at[c % acc_ref.shape[0]]
        acc[...] = jnp.dot(hn_ref[...], w_ref[:, c * chunk:(c + 1) * chunk],
                           preferred_element_type=F32)
        for hd in range(chunk // head_dim):
            col0 = c * chunk + hd * head_dim
            x = acc[:, hd * head_dim:(hd + 1) * head_dim]
            if dilation is None:
                y = jnp.where(is_rot, _rotate(x, cos_ref[0], sin_ref[0], head_dim), x)
                o_ref[:, col0:col0 + head_dim] = y.astype(o_ref.dtype)
                continue
            slab = slab_ref.at[col0 // LANES]
            slab[...] = x
            for r in range(dilation):
                xr = slab[pl.ds(r, sub_rows, stride=dilation), :] if dilation > 1 else slab[...]
                res_rows = slice(r * sub_rows, (r + 1) * sub_rows)
                y = jnp.where(is_rot,
                              _rotate(xr, cos_ref[0, 0, res_rows, :], sin_ref[0, 0, res_rows, :],
                                      head_dim), xr)
                o_ref[0, r, :, col0:col0 + head_dim] = y.astype(o_ref.dtype)


def _in_proj(h, gain, w, cos_tab, sin_tab, *, batch, seq, head_dim, n_out, col_offset=0,
             dilation=None, tm=1024, tn=1024, section=2048):
    m, k = h.shape
    seq_tiles = seq // tm
    col_tile0 = col_offset // tn
    rot_tiles = 2 * section // tn
    chunk = max(MXU_COLS, head_dim)
    scratch = [pltpu.VMEM((tm, k), BF16), pltpu.VMEM((2, tm, chunk), F32)]
    if dilation is None:
        tab_spec = pl.BlockSpec((1, tm, head_dim),
                                lambda i, j: (jnp.minimum(j * tn // section, 1), i % seq_tiles, 0))
        out_spec = pl.BlockSpec((tm, tn), lambda i, j: (i, j))
        out_shape = jax.ShapeDtypeStruct((m, n_out), BF16)
    else:
        def residue_major(tab):
            t = tab.reshape(2, seq_tiles, tm // dilation, dilation, head_dim)
            return t.transpose(0, 1, 3, 2, 4).reshape(2, seq_tiles, tm, head_dim)

        cos_tab, sin_tab = residue_major(cos_tab), residue_major(sin_tab)
        tab_spec = pl.BlockSpec((1, 1, tm, head_dim),
                                lambda i, j: (jnp.minimum(j * tn // section, 1), i % seq_tiles, 0, 0))
        out_spec = pl.BlockSpec((1, dilation, tm // dilation, tn),
                                lambda i, j: (i // seq_tiles, 0, i % seq_tiles, j))
        out_shape = jax.ShapeDtypeStruct((batch, dilation, seq // dilation, n_out), BF16)
        scratch.append(pltpu.VMEM((tn // LANES, tm, LANES), F32))
    return pl.pallas_call(
        functools.partial(_in_proj_kernel, head_dim=head_dim, rot_tiles=rot_tiles,
                          dilation=dilation),
        grid=(m // tm, n_out // tn),
        in_specs=[
            pl.BlockSpec((tm, k), lambda i, j: (i, 0)),
            pl.BlockSpec((1, k), lambda i, j: (0, 0)),
            pl.BlockSpec((k, tn), lambda i, j: (0, col_tile0 + j)),
            tab_spec,
            tab_spec,
        ],
        out_specs=out_spec,
        out_shape=out_shape,
        scratch_shapes=scratch,
        compiler_params=_params("parallel", "arbitrary"),
        name="in_proj" if dilation is None else f"in_proj_d{dilation}",
    )(h, gain.reshape(1, k), w, cos_tab, sin_tab)


A_SPAN = max(d for _, d in A_PATTERNS) * A_BLOCK
A_COMBINE_ROWS = 256


def _attn_kernel(*refs, heads):
    in_refs, out_ref, o_nat, lse_nat = refs[:-3], refs[-3], refs[-2], refs[-1]
    blk = A_BLOCK
    span = pl.program_id(1)
    qi = lax.broadcasted_iota(jnp.int32, (blk, 2 * blk), 0)
    kj = lax.broadcasted_iota(jnp.int32, (blk, 2 * blk), 1)

    for g, (window, dilation) in enumerate(A_PATTERNS):
        q_ref, kp_ref, kc_ref, vp_ref, vc_ref = in_refs[5 * g:5 * g + 5]
        n_back = window // dilation
        nq = A_SPAN // (dilation * blk)
        band = (kj >= qi + blk - n_back) & (kj <= qi + blk)
        band_first = (kj >= jnp.maximum(qi + blk - n_back, jnp.where(span > 0, 0, blk))) & (
            kj <= qi + blk)
        for r in range(dilation):
            for bq in range(nq):
                if dilation > 1:
                    rows = pl.ds(bq * blk * dilation + r, blk, stride=dilation)
                else:
                    rows = slice(bq * blk, (bq + 1) * blk)
                for h in range(heads):
                    hs = slice(h * A_HEAD_DIM, (h + 1) * A_HEAD_DIM)
                    q = q_ref[0, r, bq * blk:(bq + 1) * blk, hs]
                    if bq == 0:
                        k = jnp.concatenate([kp_ref[0, r, :, hs], kc_ref[0, r, 0:blk, hs]], axis=0)
                        v = jnp.concatenate([vp_ref[0, r, :, hs], vc_ref[0, r, 0:blk, hs]], axis=0)
                        valid = band_first
                    else:
                        k = kc_ref[0, r, (bq - 1) * blk:(bq + 1) * blk, hs]
                        v = vc_ref[0, r, (bq - 1) * blk:(bq + 1) * blk, hs]
                        valid = band
                    s = lax.dot_general(q, k, (((1,), (1,)), ((), ())),
                                        preferred_element_type=F32)
                    s = jnp.where(valid, s, -jnp.inf)
                    mx = jnp.max(s, axis=-1, keepdims=True)
                    p = jnp.exp(s - mx)
                    l = jnp.sum(p, axis=-1, keepdims=True)
                    o = jnp.dot(p.astype(BF16), v, preferred_element_type=F32)
                    o_nat[g * heads + h, rows, :] = o / l
                    lse_nat[g * heads + h, rows, :] = jnp.broadcast_to(mx + jnp.log(l), (blk, LANES))

    n_groups = len(A_PATTERNS)
    for h in range(heads):
        def combine(c, carry, h=h):
            rs = pl.ds(pl.multiple_of(c * A_COMBINE_ROWS, A_COMBINE_ROWS), A_COMBINE_ROWS)
            lses = [lse_nat[g * heads + h, rs, :] for g in range(n_groups)]
            mx = functools.reduce(jnp.maximum, lses)
            es = [jnp.exp(l - mx) for l in lses]
            num = sum(e * o_nat[g * heads + h, rs, :] for g, e in enumerate(es))
            out_ref[rs, h * A_HEAD_DIM:(h + 1) * A_HEAD_DIM] = (num / sum(es)).astype(out_ref.dtype)
            return carry
        lax.fori_loop(0, A_SPAN // A_COMBINE_ROWS, combine, 0)


def _attention(projs, *, batch, seq, heads=2):
    width = heads * A_HEAD_DIM
    sec_blocks = A_WIDTH // width
    n_spans = seq // A_SPAN
    in_specs = []
    for (window, dilation) in A_PATTERNS:
        sub_rows = A_SPAN // dilation
        nq = sub_rows // A_BLOCK

        def cur(t, dilation=dilation, sub_rows=sub_rows):
            return pl.BlockSpec((1, dilation, sub_rows, width),
                                lambda b, s, hh: (b, 0, s, t * sec_blocks + hh))

        def prev(t, dilation=dilation, nq=nq):
            return pl.BlockSpec((1, dilation, A_BLOCK, width),
                                lambda b, s, hh: (b, 0, jnp.maximum(s * nq - 1, 0),
                                                  t * sec_blocks + hh))

        in_specs += [cur(0), prev(1), cur(1), prev(2), cur(2)]
    n_slabs = len(A_PATTERNS) * heads
    return pl.pallas_call(
        functools.partial(_attn_kernel, heads=heads),
        grid=(batch, n_spans, sec_blocks),
        in_specs=in_specs,
        out_specs=pl.BlockSpec((A_SPAN, width), lambda b, s, hh: (b * n_spans + s, hh)),
        out_shape=jax.ShapeDtypeStruct((batch * seq, A_WIDTH), BF16),
        scratch_shapes=[pltpu.VMEM((n_slabs, A_SPAN, LANES), F32),
                        pltpu.VMEM((n_slabs, A_SPAN, LANES), F32)],
        compiler_params=_params("parallel", "parallel", "parallel"),
        name="attention",
    )(*[p for p in projs for _ in range(5)])


def _retention_kernel(q_ref, k_ref, v_ref, g_ref, decay_ref, qdec_ref, kdec_ref, cdec_ref,
                      o_ref, state_ref, *, chunks):
    c_len = R_CHUNK

    @pl.when(pl.program_id(2) == 0)
    def _():
        state_ref[...] = jnp.zeros_like(state_ref)

    decay = decay_ref[0]
    qdec = qdec_ref[0]
    kdec = kdec_ref[0]
    cdec = cdec_ref[0]
    for c in range(chunks):
        rows = slice(c * c_len, (c + 1) * c_len)
        q = q_ref[rows, :]
        k = k_ref[rows, :]
        v = v_ref[rows, :]
        scores = lax.dot_general(q, k, (((1,), (1,)), ((), ())), preferred_element_type=F32) * decay
        y = jnp.dot(scores.astype(BF16), v, preferred_element_type=F32)
        state = state_ref[...]
        y = y + jnp.dot(q, state.astype(BF16), preferred_element_type=F32) * qdec
        k_dec = (k.astype(F32) * kdec).astype(BF16)
        state_ref[...] = state * cdec + lax.dot_general(
            k_dec, v, (((0,), (0,)), ((), ())), preferred_element_type=F32)
        mu = jnp.mean(y, axis=-1, keepdims=True)
        yc = y - mu
        var = jnp.mean(yc * yc, axis=-1, keepdims=True)
        yn = yc * lax.rsqrt(var + EPS)
        g = g_ref[rows, :].astype(F32)
        o_ref[rows, :] = (g / (1.0 + jnp.exp(-g)) * yn).astype(o_ref.dtype)


def _retention_tables():
    heads = R_HEADS
    c_len = R_CHUNK
    log_gamma = jnp.log1p(-jnp.exp2(-5.0 - jnp.arange(heads, dtype=F32)))
    idx = jnp.arange(c_len, dtype=F32)
    rel = idx[:, None] - idx[None, :]
    decay = jnp.where(rel >= 0, jnp.exp(log_gamma[:, None, None] * jnp.maximum(rel, 0.0)), 0.0)
    q_decay = jnp.exp(log_gamma[None, :] * (idx[:, None] + 1.0))
    k_decay = jnp.exp(log_gamma[None, :] * (c_len - 1.0 - idx[:, None]))
    chunk_decay = jnp.exp(log_gamma * c_len)
    qdec = jnp.broadcast_to(q_decay.T[:, :, None], (heads, c_len, R_V_DIM))
    kdec = jnp.broadcast_to(k_decay.T[:, :, None], (heads, c_len, R_QK_DIM))
    cdec = jnp.broadcast_to(chunk_decay[:, None, None], (heads, 1, R_V_DIM))
    return decay, qdec, kdec, cdec


def _retention(proj, *, batch, seq, chunks=8):
    m = proj.shape[0]
    rows = chunks * R_CHUNK
    steps = seq // rows
    decay, qdec, kdec, cdec = _retention_tables()
    k_off = R_QK_WIDTH // R_QK_DIM
    v_off = 2 * R_QK_WIDTH // R_V_DIM
    g_off = (2 * R_QK_WIDTH + R_V_WIDTH) // R_V_DIM

    def tok(off):
        return lambda b, h, t: (b * steps + t, off + h)

    def per_head(b, h, t):
        return (h, 0, 0)

    return pl.pallas_call(
        functools.partial(_retention_kernel, chunks=chunks),
        grid=(batch, R_HEADS, steps),
        in_specs=[
            pl.BlockSpec((rows, R_QK_DIM), tok(0)),
            pl.BlockSpec((rows, R_QK_DIM), tok(k_off)),
            pl.BlockSpec((rows, R_V_DIM), tok(v_off)),
            pl.BlockSpec((rows, R_V_DIM), tok(g_off)),
            pl.BlockSpec((1, R_CHUNK, R_CHUNK), per_head),
            pl.BlockSpec((1, R_CHUNK, R_V_DIM), per_head),
            pl.BlockSpec((1, R_CHUNK, R_QK_DIM), per_head),
            pl.BlockSpec((1, 1, R_V_DIM), per_head),
        ],
        out_specs=pl.BlockSpec((rows, R_V_DIM), tok(0)),
        out_shape=jax.ShapeDtypeStruct((m, R_V_WIDTH), BF16),
        scratch_shapes=[pltpu.VMEM((R_QK_DIM, R_V_DIM), F32)],
        compiler_params=_params("parallel", "parallel", "arbitrary"),
        name="retention",
    )(proj, proj, proj, proj, decay, qdec, kdec, cdec)


def _ffn_up_kernel(h_ref, halo_ref, g_ref, wg_ref, wu_ref, cwg_ref, cwu_ref, cbg_ref, cbu_ref,
                   o_ref, hn_ref, ug_ref, uu_ref, *, tiles_per_seq):
    tm = h_ref.shape[0]
    pad = halo_ref.shape[0]
    tn = o_ref.shape[1]
    chunk = ug_ref.shape[-1]

    @pl.when(pl.program_id(1) == 0)
    def _():
        at_seq_start = pl.program_id(0) % tiles_per_seq == 0
        halo = _rms_scale(halo_ref[...], g_ref[...])
        hn_ref[0:pad, :] = jnp.where(at_seq_start, 0.0, halo).astype(hn_ref.dtype)
        _rms_rows_to(hn_ref, pad, h_ref, g_ref[...])

    for c in range(tn // chunk):
        cols = slice(c * chunk, (c + 1) * chunk)
        slot = c % ug_ref.shape[0]

        def conv(u_ref, w_ref, cw_ref, cb_ref):
            u = u_ref.at[slot]
            u[...] = jnp.dot(hn_ref[...], w_ref[:, cols], preferred_element_type=F32)
            cw = cw_ref[:, cols]
            y = cb_ref[:, cols] + cw[CONV_WIDTH - 1:CONV_WIDTH] * u[pad:pad + tm, :]
            for back in range(1, CONV_WIDTH):
                tap = CONV_WIDTH - 1 - back
                y = y + cw[tap:tap + 1] * u[pad - back:pad - back + tm, :]
            return y

        gate = conv(ug_ref, wg_ref, cwg_ref, cbg_ref)
        up = conv(uu_ref, wu_ref, cwu_ref, cbu_ref)
        o_ref[:, cols] = (gate / (1.0 + jnp.exp(-gate)) * up).astype(o_ref.dtype)


def _ffn_up(h, gain, w_up, conv_w, conv_b, *, seq, tm=1024, tn=512):
    m, k = h.shape
    pad = BF16_ROWS_PER_TILE
    n_tiles = D_FF // tn
    halo_blocks = tm // pad

    def halo_map(i, j):
        return (jnp.maximum(i * halo_blocks - 1, 0), 0)

    conv_b = conv_b.reshape(1, 2 * D_FF)
    return pl.pallas_call(
        functools.partial(_ffn_up_kernel, tiles_per_seq=seq // tm),
        grid=(m // tm, n_tiles),
        in_specs=[
            pl.BlockSpec((tm, k), lambda i, j: (i, 0)),
            pl.BlockSpec((pad, k), halo_map),
            pl.BlockSpec((1, k), lambda i, j: (0, 0)),
            pl.BlockSpec((k, tn), lambda i, j: (0, j)),
            pl.BlockSpec((k, tn), lambda i, j: (0, n_tiles + j)),
            pl.BlockSpec((CONV_WIDTH, tn), lambda i, j: (0, j)),
            pl.BlockSpec((CONV_WIDTH, tn), lambda i, j: (0, n_tiles + j)),
            pl.BlockSpec((1, tn), lambda i, j: (0, j)),
            pl.BlockSpec((1, tn), lambda i, j: (0, n_tiles + j)),
        ],
        out_specs=pl.BlockSpec((tm, tn), lambda i, j: (i, j)),
        out_shape=jax.ShapeDtypeStruct((m, D_FF), BF16),
        scratch_shapes=[pltpu.VMEM((pad + tm, k), BF16),
                        pltpu.VMEM((2, pad + tm, MXU_COLS), F32),
                        pltpu.VMEM((2, pad + tm, MXU_COLS), F32)],
        compiler_params=_params("parallel", "arbitrary"),
        name="ffn_up",
    )(h, h, gain.reshape(1, k), w_up, w_up, conv_w, conv_w, conv_b, conv_b)


def _out_proj_kernel(a_ref, w_ref, h_ref, o_ref):
    for c in range(o_ref.shape[1] // MXU_COLS):
        cols = slice(c * MXU_COLS, (c + 1) * MXU_COLS)
        o_ref[:, cols] = h_ref[:, cols] + jnp.dot(a_ref[...], w_ref[:, cols],
                                                  preferred_element_type=F32)


def _out_proj(a, w, h, *, tm=1024, tn=512):
    m, k = a.shape
    d = w.shape[1]
    tile = pl.BlockSpec((tm, tn), lambda i, j: (i, j))
    return pl.pallas_call(
        _out_proj_kernel,
        grid=(m // tm, d // tn),
        in_specs=[
            pl.BlockSpec((tm, k), lambda i, j: (i, 0)),
            pl.BlockSpec((k, tn), lambda i, j: (0, j)),
            tile,
        ],
        out_specs=tile,
        out_shape=jax.ShapeDtypeStruct((m, d), F32),
        compiler_params=_params("parallel", "arbitrary"),
        name="out_proj",
    )(a, w, h)


def _rotary_tables(seq, inv_freq, q_scale, k_scale):
    ang = jnp.arange(seq, dtype=F32)[:, None] * inv_freq[None, :]
    cos, sin = jnp.cos(ang), jnp.sin(ang)
    cos_full = jnp.concatenate([cos, cos], axis=-1)
    sin_signed = jnp.concatenate([-sin, sin], axis=-1)
    scales = jnp.array([q_scale, k_scale], F32)[:, None, None]
    return cos_full[None] * scales, sin_signed[None] * scales


def kernel(x, norm_mix, norm_ffn, norm_final, w_in_attn, w_out_attn, w_in_ret, w_out_ret,
           w_up, conv_w, conv_b, w_down):
    batch, seq, d = x.shape
    depth = norm_mix.shape[0]
    inv_freq_a = ROPE_THETA ** (-jnp.arange(0, A_HEAD_DIM, 2, dtype=F32) / A_HEAD_DIM)
    cos_a, sin_a = _rotary_tables(seq, inv_freq_a, A_HEAD_DIM ** -0.5, 1.0)
    inv_freq_r = ROPE_THETA ** (-jnp.linspace(0.0, 1.0, R_QK_DIM // 2, dtype=F32))
    cos_r, sin_r = _rotary_tables(seq, inv_freq_r, 1.0, R_QK_DIM ** -0.5)

    h = x.reshape(batch * seq, d)
    for i in range(depth):
        li = i // 2
        if i % 2 == 0:
            w_in = w_in_attn[li].astype(BF16)
            group_cols = w_in.shape[1] // A_N_GROUPS
            projs = [
                _in_proj(h, norm_mix[i], w_in, cos_a, sin_a, batch=batch, seq=seq,
                         head_dim=A_HEAD_DIM, n_out=group_cols, col_offset=g * group_cols,
                         dilation=dilation)
                for g, (_, dilation) in enumerate(A_PATTERNS)
            ]
            mixed = _attention(projs, batch=batch, seq=seq)
            h = _out_proj(mixed, w_out_attn[li].astype(BF16), h, tn=1024)
        else:
            w_in = w_in_ret[li].astype(BF16)
            proj = _in_proj(h, norm_mix[i], w_in, cos_r, sin_r, batch=batch, seq=seq,
                            head_dim=R_QK_DIM, n_out=w_in.shape[1])
            mixed = _retention(proj, batch=batch, seq=seq)
            h = _out_proj(mixed, w_out_ret[li].astype(BF16), h)
        act = _ffn_up(h, norm_ffn[i], w_up[i].astype(BF16), conv_w[i], conv_b[i], seq=seq)
        h = _out_proj(act, w_down[i].astype(BF16), h)
    return _rmsnorm(h, norm_final).reshape(batch, seq, d)
```

```python
import functools

import jax
import jax.numpy as jnp
from jax import lax
from jax.experimental import pallas as pl
from jax.experimental.pallas import tpu as pltpu

F32 = jnp.float32
BF16 = jnp.bfloat16

D_MODEL = 2048
EPS = 1e-6
ROPE_THETA = 10000.0

A_HEAD_DIM = 128
A_HEADS = D_MODEL // A_HEAD_DIM
A_WIDTH = A_HEADS * A_HEAD_DIM
A_PATTERNS = ((128, 1), (512, 4), (2048, 16))
A_N_GROUPS = len(A_PATTERNS)
A_BLOCK = 128

R_QK_DIM = 256
R_V_DIM = 512
R_HEADS = D_MODEL // R_QK_DIM
R_QK_WIDTH = R_HEADS * R_QK_DIM
R_V_WIDTH = R_HEADS * R_V_DIM
R_CHUNK = 128
R_HEAD_COLS = 2 * R_QK_DIM + 2 * R_V_DIM
A_HEADS_PER_STEP = 2

D_FF = 5632
CONV_WIDTH = 3

BF16_ROWS_PER_TILE = 16
LANES = 128
MXU_COLS = 256
NORM_ROWS = 256
VMEM_LIMIT_BYTES = 56 * 1024 * 1024


def _params(*semantics):
    return pltpu.CompilerParams(dimension_semantics=semantics, vmem_limit_bytes=VMEM_LIMIT_BYTES)


def _rms_scale(h, gain):
    return h * lax.rsqrt(jnp.mean(h * h, axis=-1, keepdims=True) + EPS) * gain


def _rms_rows_to(dst_ref, dst_row0, src_ref, gain, rows_per_pass=NORM_ROWS):
    n_rows = src_ref.shape[0]
    for r0 in range(0, n_rows, rows_per_pass):
        r1 = min(r0 + rows_per_pass, n_rows)
        dst_ref[dst_row0 + r0:dst_row0 + r1, :] = _rms_scale(src_ref[r0:r1, :], gain).astype(
            dst_ref.dtype)


def _rmsnorm_kernel(x_ref, g_ref, o_ref):
    o_ref[...] = _rms_scale(x_ref[...], g_ref[...]).astype(o_ref.dtype)


def _rmsnorm(x, gain, tm=512):
    m, d = x.shape
    return pl.pallas_call(
        _rmsnorm_kernel,
        grid=(m // tm,),
        in_specs=[pl.BlockSpec((tm, d), lambda i: (i, 0)), pl.BlockSpec((1, d), lambda i: (0, 0))],
        out_specs=pl.BlockSpec((tm, d), lambda i: (i, 0)),
        out_shape=jax.ShapeDtypeStruct((m, d), x.dtype),
        compiler_params=_params("parallel"),
        name="rmsnorm",
    )(x, gain.reshape(1, d))


def _rotate(x, cos, sin, head_dim):
    return x * cos + pltpu.roll(x, head_dim // 2, 1) * sin


def _in_proj_kernel(h_ref, g_ref, w_ref, cos_ref, sin_ref, o_ref, hn_ref, acc_ref, *scratch,
                    head_dim, chunk_tables, dilation):
    @pl.when(pl.program_id(1) == 0)
    def _():
        _rms_rows_to(hn_ref, 0, h_ref, g_ref[...])

    tm = hn_ref.shape[0]
    chunk = acc_ref.shape[-1]
    if dilation is not None:
        assert head_dim == LANES
        (slab_ref,) = scratch
        sub_rows = tm // dilation

    for c, table in enumerate(chunk_tables):
        acc = acc_ref.at[c % acc_ref.shape[0]]
        acc[...] = jnp.dot(hn_ref[...], w_ref[:, c * chunk:(c + 1) * chunk],
                           preferred_element_type=F32)
        for hd in range(chunk // head_dim):
            col0 = c * chunk + hd * head_dim
            x = acc[:, hd * head_dim:(hd + 1) * head_dim]
            if dilation is None:
                if table is not None:
                    x = _rotate(x, cos_ref[table], sin_ref[table], head_dim)
                o_ref[:, col0:col0 + head_dim] = x.astype(o_ref.dtype)
                continue
            slab = slab_ref.at[col0 // LANES]
            slab[...] = x
            for r in range(dilation):
                xr = slab[pl.ds(r, sub_rows, stride=dilation), :] if dilation > 1 else slab[...]
                if table is not None:
                    res_rows = slice(r * sub_rows, (r + 1) * sub_rows)
                    xr = _rotate(xr, cos_ref[table, 0, res_rows, :], sin_ref[table, 0, res_rows, :],
                                 head_dim)
                o_ref[0, r, :, col0:col0 + head_dim] = xr.astype(o_ref.dtype)


def _in_proj(h, gain, w, cos_tab, sin_tab, *, batch, seq, head_dim, n_out, tile_widths,
             col_offset=0, dilation=None, tm=1024):
    m, k = h.shape
    seq_tiles = seq // tm
    chunk = max(MXU_COLS, head_dim)
    q_w, k_w, rest_w = tile_widths
    tn = q_w + k_w + rest_w
    chunk_tables = (0,) * (q_w // chunk) + (1,) * (k_w // chunk) + (None,) * (rest_w // chunk)
    col_tile0 = col_offset // tn
    scratch = [pltpu.VMEM((tm, k), BF16), pltpu.VMEM((2, tm, chunk), F32)]
    if dilation is None:
        tab_spec = pl.BlockSpec((2, tm, head_dim), lambda i, j: (0, i % seq_tiles, 0))
        out_spec = pl.BlockSpec((tm, tn), lambda i, j: (i, j))
        out_shape = jax.ShapeDtypeStruct((m, n_out), BF16)
    else:
        def residue_major(tab):
            t = tab.reshape(2, seq_tiles, tm // dilation, dilation, head_dim)
            return t.transpose(0, 1, 3, 2, 4).reshape(2, seq_tiles, tm, head_dim)

        cos_tab, sin_tab = residue_major(cos_tab), residue_major(sin_tab)
        tab_spec = pl.BlockSpec((2, 1, tm, head_dim), lambda i, j: (0, i % seq_tiles, 0, 0))
        out_spec = pl.BlockSpec((1, dilation, tm // dilation, tn),
                                lambda i, j: (i // seq_tiles, 0, i % seq_tiles, j))
        out_shape = jax.ShapeDtypeStruct((batch, dilation, seq // dilation, n_out), BF16)
        scratch.append(pltpu.VMEM((tn // LANES, tm, LANES), F32))
    return pl.pallas_call(
        functools.partial(_in_proj_kernel, head_dim=head_dim, chunk_tables=chunk_tables,
                          dilation=dilation),
        grid=(m // tm, n_out // tn),
        in_specs=[
            pl.BlockSpec((tm, k), lambda i, j: (i, 0)),
            pl.BlockSpec((1, k), lambda i, j: (0, 0)),
            pl.BlockSpec((k, tn), lambda i, j: (0, col_tile0 + j)),
            tab_spec,
            tab_spec,
        ],
        out_specs=out_spec,
        out_shape=out_shape,
        scratch_shapes=scratch,
        compiler_params=_params("parallel", "arbitrary"),
        name="in_proj" if dilation is None else f"in_proj_d{dilation}",
    )(h, gain.reshape(1, k), w, cos_tab, sin_tab)


A_SPAN = max(d for _, d in A_PATTERNS) * A_BLOCK
A_COMBINE_ROWS = 256


def _attn_kernel(*refs, heads):
    in_refs, out_ref, o_nat, lse_nat = refs[:-3], refs[-3], refs[-2], refs[-1]
    blk = A_BLOCK
    width = heads * A_HEAD_DIM
    span = pl.program_id(1)
    qi = lax.broadcasted_iota(jnp.int32, (blk, 2 * blk), 0)
    kj = lax.broadcasted_iota(jnp.int32, (blk, 2 * blk), 1)

    for g, (window, dilation) in enumerate(A_PATTERNS):
        cur_ref, prev_ref = in_refs[2 * g:2 * g + 2]
        n_back = window // dilation
        nq = A_SPAN // (dilation * blk)
        band = (kj >= qi + blk - n_back) & (kj <= qi + blk)
        band_first = (kj >= jnp.maximum(qi + blk - n_back, jnp.where(span > 0, 0, blk))) & (
            kj <= qi + blk)
        for r in range(dilation):
            for bq in range(nq):
                if dilation > 1:
                    rows = pl.ds(bq * blk * dilation + r, blk, stride=dilation)
                else:
                    rows = slice(bq * blk, (bq + 1) * blk)
                for h in range(heads):
                    qs, ks, vs = (slice(t * width + h * A_HEAD_DIM, t * width + (h + 1) * A_HEAD_DIM)
                                  for t in range(3))
                    q = cur_ref[0, r, bq * blk:(bq + 1) * blk, qs]
                    if bq == 0:
                        k = jnp.concatenate([prev_ref[0, r, :, ks], cur_ref[0, r, 0:blk, ks]], axis=0)
                        v = jnp.concatenate([prev_ref[0, r, :, vs], cur_ref[0, r, 0:blk, vs]], axis=0)
                        valid = band_first
                    else:
                        k = cur_ref[0, r, (bq - 1) * blk:(bq + 1) * blk, ks]
                        v = cur_ref[0, r, (bq - 1) * blk:(bq + 1) * blk, vs]
                        valid = band
                    s = lax.dot_general(q, k, (((1,), (1,)), ((), ())),
                                        preferred_element_type=F32)
                    s = jnp.where(valid, s, -jnp.inf)
                    mx = jnp.max(s, axis=-1, keepdims=True)
                    p = jnp.exp(s - mx)
                    l = jnp.sum(p, axis=-1, keepdims=True)
                    o = jnp.dot(p.astype(BF16), v, preferred_element_type=F32)
                    o_nat[g * heads + h, rows, :] = o / l
                    lse_nat[g * heads + h, rows, :] = jnp.broadcast_to(mx + jnp.log(l), (blk, LANES))

    n_groups = len(A_PATTERNS)
    for h in range(heads):
        def combine(c, carry, h=h):
            rs = pl.ds(pl.multiple_of(c * A_COMBINE_ROWS, A_COMBINE_ROWS), A_COMBINE_ROWS)
            lses = [lse_nat[g * heads + h, rs, :] for g in range(n_groups)]
            mx = functools.reduce(jnp.maximum, lses)
            es = [jnp.exp(l - mx) for l in lses]
            num = sum(e * o_nat[g * heads + h, rs, :] for g, e in enumerate(es))
            out_ref[rs, h * A_HEAD_DIM:(h + 1) * A_HEAD_DIM] = (num / sum(es)).astype(out_ref.dtype)
            return carry
        lax.fori_loop(0, A_SPAN // A_COMBINE_ROWS, combine, 0)


def _attention(projs, *, batch, seq, heads=2):
    width = heads * A_HEAD_DIM
    sec_blocks = A_WIDTH // width
    n_spans = seq // A_SPAN
    in_specs = []
    for (window, dilation) in A_PATTERNS:
        sub_rows = A_SPAN // dilation
        nq = sub_rows // A_BLOCK
        in_specs.append(pl.BlockSpec((1, dilation, sub_rows, 3 * width),
                                     lambda b, s, hh: (b, 0, s, hh)))
        in_specs.append(pl.BlockSpec((1, dilation, A_BLOCK, 3 * width),
                                     lambda b, s, hh, nq=nq: (b, 0, jnp.maximum(s * nq - 1, 0), hh)))
    n_slabs = len(A_PATTERNS) * heads
    return pl.pallas_call(
        functools.partial(_attn_kernel, heads=heads),
        grid=(batch, n_spans, sec_blocks),
        in_specs=in_specs,
        out_specs=pl.BlockSpec((A_SPAN, width), lambda b, s, hh: (b * n_spans + s, hh)),
        out_shape=jax.ShapeDtypeStruct((batch * seq, A_WIDTH), BF16),
        scratch_shapes=[pltpu.VMEM((n_slabs, A_SPAN, LANES), F32),
                        pltpu.VMEM((n_slabs, A_SPAN, LANES), F32)],
        compiler_params=_params("parallel", "parallel", "parallel"),
        name="attention",
    )(*[p for p in projs for _ in range(2)])


def _retention_kernel(p_ref, decay_ref, qdec_ref, kdec_ref, cdec_ref, o_ref, state_ref, *, chunks):
    c_len = R_CHUNK
    q_cols = slice(0, R_QK_DIM)
    k_cols = slice(R_QK_DIM, 2 * R_QK_DIM)
    v_cols = slice(2 * R_QK_DIM, 2 * R_QK_DIM + R_V_DIM)
    g_cols = slice(2 * R_QK_DIM + R_V_DIM, R_HEAD_COLS)

    @pl.when(pl.program_id(2) == 0)
    def _():
        state_ref[...] = jnp.zeros_like(state_ref)

    decay = decay_ref[0]
    qdec = qdec_ref[0]
    kdec = kdec_ref[0]
    cdec = cdec_ref[0]
    for c in range(chunks):
        rows = slice(c * c_len, (c + 1) * c_len)
        q = p_ref[rows, q_cols]
        k = p_ref[rows, k_cols]
        v = p_ref[rows, v_cols]
        scores = lax.dot_general(q, k, (((1,), (1,)), ((), ())), preferred_element_type=F32) * decay
        y = jnp.dot(scores.astype(BF16), v, preferred_element_type=F32)
        state = state_ref[...]
        y = y + jnp.dot(q, state.astype(BF16), preferred_element_type=F32) * qdec
        k_dec = (k.astype(F32) * kdec).astype(BF16)
        state_ref[...] = state * cdec + lax.dot_general(
            k_dec, v, (((0,), (0,)), ((), ())), preferred_element_type=F32)
        mu = jnp.mean(y, axis=-1, keepdims=True)
        yc = y - mu
        var = jnp.mean(yc * yc, axis=-1, keepdims=True)
        yn = yc * lax.rsqrt(var + EPS)
        g = p_ref[rows, g_cols].astype(F32)
        o_ref[rows, :] = (g / (1.0 + jnp.exp(-g)) * yn).astype(o_ref.dtype)


def _retention_tables():
    heads = R_HEADS
    c_len = R_CHUNK
    log_gamma = jnp.log1p(-jnp.exp2(-5.0 - jnp.arange(heads, dtype=F32)))
    idx = jnp.arange(c_len, dtype=F32)
    rel = idx[:, None] - idx[None, :]
    decay = jnp.where(rel >= 0, jnp.exp(log_gamma[:, None, None] * jnp.maximum(rel, 0.0)), 0.0)
    q_decay = jnp.exp(log_gamma[None, :] * (idx[:, None] + 1.0))
    k_decay = jnp.exp(log_gamma[None, :] * (c_len - 1.0 - idx[:, None]))
    chunk_decay = jnp.exp(log_gamma * c_len)
    qdec = jnp.broadcast_to(q_decay.T[:, :, None], (heads, c_len, R_V_DIM))
    kdec = jnp.broadcast_to(k_decay.T[:, :, None], (heads, c_len, R_QK_DIM))
    cdec = jnp.broadcast_to(chunk_decay[:, None, None], (heads, 1, R_V_DIM))
    return decay, qdec, kdec, cdec


def _retention(proj, *, batch, seq, chunks=8):
    m = proj.shape[0]
    rows = chunks * R_CHUNK
    steps = seq // rows
    decay, qdec, kdec, cdec = _retention_tables()

    def tok(b, h, t):
        return (b * steps + t, h)

    def per_head(b, h, t):
        return (h, 0, 0)

    return pl.pallas_call(
        functools.partial(_retention_kernel, chunks=chunks),
        grid=(batch, R_HEADS, steps),
        in_specs=[
            pl.BlockSpec((rows, R_HEAD_COLS), tok),
            pl.BlockSpec((1, R_CHUNK, R_CHUNK), per_head),
            pl.BlockSpec((1, R_CHUNK, R_V_DIM), per_head),
            pl.BlockSpec((1, R_CHUNK, R_QK_DIM), per_head),
            pl.BlockSpec((1, 1, R_V_DIM), per_head),
        ],
        out_specs=pl.BlockSpec((rows, R_V_DIM), tok),
        out_shape=jax.ShapeDtypeStruct((m, R_V_WIDTH), BF16),
        scratch_shapes=[pltpu.VMEM((R_QK_DIM, R_V_DIM), F32)],
        compiler_params=_params("parallel", "parallel", "arbitrary"),
        name="retention",
    )(proj, decay, qdec, kdec, cdec)


def _ffn_up_kernel(h_ref, halo_ref, g_ref, wg_ref, wu_ref, cwg_ref, cwu_ref, cbg_ref, cbu_ref,
                   o_ref, hn_ref, ug_ref, uu_ref, *, tiles_per_seq):
    tm = h_ref.shape[0]
    pad = halo_ref.shape[0]
    tn = o_ref.shape[1]
    chunk = ug_ref.shape[-1]

    @pl.when(pl.program_id(1) == 0)
    def _():
        at_seq_start = pl.program_id(0) % tiles_per_seq == 0
        halo = _rms_scale(halo_ref[...], g_ref[...])
        hn_ref[0:pad, :] = jnp.where(at_seq_start, 0.0, halo).astype(hn_ref.dtype)
        _rms_rows_to(hn_ref, pad, h_ref, g_ref[...])

    for c in range(tn // chunk):
        cols = slice(c * chunk, (c + 1) * chunk)
        slot = c % ug_ref.shape[0]

        def conv(u_ref, w_ref, cw_ref, cb_ref):
            u = u_ref.at[slot]
            u[...] = jnp.dot(hn_ref[...], w_ref[:, cols], preferred_element_type=F32)
            cw = cw_ref[:, cols]
            y = cb_ref[:, cols] + cw[CONV_WIDTH - 1:CONV_WIDTH] * u[pad:pad + tm, :]
            for back in range(1, CONV_WIDTH):
                tap = CONV_WIDTH - 1 - back
                y = y + cw[tap:tap + 1] * u[pad - back:pad - back + tm, :]
            return y

        gate = conv(ug_ref, wg_ref, cwg_ref, cbg_ref)
        up = conv(uu_ref, wu_ref, cwu_ref, cbu_ref)
        o_ref[:, cols] = (gate / (1.0 + jnp.exp(-gate)) * up).astype(o_ref.dtype)


def _ffn_up(h, gain, w_up, conv_w, conv_b, *, seq, tm=1024, tn=512):
    m, k = h.shape
    pad = BF16_ROWS_PER_TILE
    n_tiles = D_FF // tn
    halo_blocks = tm // pad

    def halo_map(i, j):
        return (jnp.maximum(i * halo_blocks - 1, 0), 0)

    conv_b = conv_b.reshape(1, 2 * D_FF)
    return pl.pallas_call(
        functools.partial(_ffn_up_kernel, tiles_per_seq=seq // tm),
        grid=(m // tm, n_tiles),
        in_specs=[
            pl.BlockSpec((tm, k), lambda i, j: (i, 0)),
            pl.BlockSpec((pad, k), halo_map),
            pl.BlockSpec((1, k), lambda i, j: (0, 0)),
            pl.BlockSpec((k, tn), lambda i, j: (0, j)),
            pl.BlockSpec((k, tn), lambda i, j: (0, n_tiles + j)),
            pl.BlockSpec((CONV_WIDTH, tn), lambda i, j: (0, j)),
            pl.BlockSpec((CONV_WIDTH, tn), lambda i, j: (0, n_tiles + j)),
            pl.BlockSpec((1, tn), lambda i, j: (0, j)),
            pl.BlockSpec((1, tn), lambda i, j: (0, n_tiles + j)),
        ],
        out_specs=pl.BlockSpec((tm, tn), lambda i, j: (i, j)),
        out_shape=jax.ShapeDtypeStruct((m, D_FF), BF16),
        scratch_shapes=[pltpu.VMEM((pad + tm, k), BF16),
                        pltpu.VMEM((2, pad + tm, MXU_COLS), F32),
                        pltpu.VMEM((2, pad + tm, MXU_COLS), F32)],
        compiler_params=_params("parallel", "arbitrary"),
        name="ffn_up",
    )(h, h, gain.reshape(1, k), w_up, w_up, conv_w, conv_w, conv_b, conv_b)


def _out_proj_kernel(a_ref, w_ref, h_ref, o_ref):
    for c in range(o_ref.shape[1] // MXU_COLS):
        cols = slice(c * MXU_COLS, (c + 1) * MXU_COLS)
        o_ref[:, cols] = h_ref[:, cols] + jnp.dot(a_ref[...], w_ref[:, cols],
                                                  preferred_element_type=F32)


def _out_proj(a, w, h, *, tm=1024, tn=512):
    m, k = a.shape
    d = w.shape[1]
    tile = pl.BlockSpec((tm, tn), lambda i, j: (i, j))
    return pl.pallas_call(
        _out_proj_kernel,
        grid=(m // tm, d // tn),
        in_specs=[
            pl.BlockSpec((tm, k), lambda i, j: (i, 0)),
            pl.BlockSpec((k, tn), lambda i, j: (0, j)),
            tile,
        ],
        out_specs=tile,
        out_shape=jax.ShapeDtypeStruct((m, d), F32),
        compiler_params=_params("parallel", "arbitrary"),
        name="out_proj",
    )(a, w, h)


def _rotary_tables(seq, inv_freq, q_scale, k_scale):
    ang = jnp.arange(seq, dtype=F32)[:, None] * inv_freq[None, :]
    cos, sin = jnp.cos(ang), jnp.sin(ang)
    cos_full = jnp.concatenate([cos, cos], axis=-1)
    sin_signed = jnp.concatenate([-sin, sin], axis=-1)
    scales = jnp.array([q_scale, k_scale], F32)[:, None, None]
    return cos_full[None] * scales, sin_signed[None] * scales


def _attn_weight_by_head_block(w):
    k = w.shape[0]
    blocks = A_HEADS // A_HEADS_PER_STEP
    w = w.reshape(k, A_N_GROUPS, 3, blocks, A_HEADS_PER_STEP * A_HEAD_DIM)
    return w.transpose(0, 1, 3, 2, 4).reshape(k, -1)


def _ret_weight_by_head(w):
    k = w.shape[0]
    q, kk, v, g = jnp.split(w, [R_QK_WIDTH, 2 * R_QK_WIDTH, 2 * R_QK_WIDTH + R_V_WIDTH], axis=1)
    parts = [q.reshape(k, R_HEADS, R_QK_DIM), kk.reshape(k, R_HEADS, R_QK_DIM),
             v.reshape(k, R_HEADS, R_V_DIM), g.reshape(k, R_HEADS, R_V_DIM)]
    return jnp.concatenate(parts, axis=2).reshape(k, -1)


def kernel(x, norm_mix, norm_ffn, norm_final, w_in_attn, w_out_attn, w_in_ret, w_out_ret,
           w_up, conv_w, conv_b, w_down):
    batch, seq, d = x.shape
    depth = norm_mix.shape[0]
    inv_freq_a = ROPE_THETA ** (-jnp.arange(0, A_HEAD_DIM, 2, dtype=F32) / A_HEAD_DIM)
    cos_a, sin_a = _rotary_tables(seq, inv_freq_a, A_HEAD_DIM ** -0.5, 1.0)
    inv_freq_r = ROPE_THETA ** (-jnp.linspace(0.0, 1.0, R_QK_DIM // 2, dtype=F32))
    cos_r, sin_r = _rotary_tables(seq, inv_freq_r, 1.0, R_QK_DIM ** -0.5)

    h = x.reshape(batch * seq, d)
    for i in range(depth):
        li = i // 2
        if i % 2 == 0:
            w_in = _attn_weight_by_head_block(w_in_attn[li]).astype(BF16)
            group_cols = w_in.shape[1] // A_N_GROUPS
            block_w = A_HEADS_PER_STEP * A_HEAD_DIM
            projs = [
                _in_proj(h, norm_mix[i], w_in, cos_a, sin_a, batch=batch, seq=seq,
                         head_dim=A_HEAD_DIM, n_out=group_cols, col_offset=g * group_cols,
                         tile_widths=(block_w, block_w, block_w), dilation=dilation)
                for g, (_, dilation) in enumerate(A_PATTERNS)
            ]
            mixed = _attention(projs, batch=batch, seq=seq, heads=A_HEADS_PER_STEP)
            h = _out_proj(mixed, w_out_attn[li].astype(BF16), h, tn=1024)
        else:
            w_in = _ret_weight_by_head(w_in_ret[li]).astype(BF16)
            proj = _in_proj(h, norm_mix[i], w_in, cos_r, sin_r, batch=batch, seq=seq,
                            head_dim=R_QK_DIM, n_out=w_in.shape[1],
                            tile_widths=(R_QK_DIM, R_QK_DIM, 2 * R_V_DIM))
            mixed = _retention(proj, batch=batch, seq=seq)
            h = _out_proj(mixed, w_out_ret[li].astype(BF16), h)
        act = _ffn_up(h, norm_ffn[i], w_up[i].astype(BF16), conv_w[i], conv_b[i], seq=seq)
        h = _out_proj(act, w_down[i].astype(BF16), h)
    return _rmsnorm(h, norm_final).reshape(batch, seq, d)
```

```python
import functools

import jax
import jax.numpy as jnp
from jax import lax
from jax.experimental import pallas as pl
from jax.experimental.pallas import tpu as pltpu

F32 = jnp.float32
BF16 = jnp.bfloat16

D_MODEL = 2048
EPS = 1e-6
ROPE_THETA = 10000.0

A_HEAD_DIM = 128
A_HEADS = D_MODEL // A_HEAD_DIM
A_WIDTH = A_HEADS * A_HEAD_DIM
A_PATTERNS = ((128, 1), (512, 4), (2048, 16))
A_N_GROUPS = len(A_PATTERNS)
A_BLOCK = 128

R_QK_DIM = 256
R_V_DIM = 512
R_HEADS = D_MODEL // R_QK_DIM
R_QK_WIDTH = R_HEADS * R_QK_DIM
R_V_WIDTH = R_HEADS * R_V_DIM
R_CHUNK = 128

D_FF = 5632
CONV_WIDTH = 3

BF16_ROWS_PER_TILE = 16
F32_ROWS_PER_TILE = 8
LANES = 128
MXU_COLS = 256
SINGLE_OP_STRIDE = 4
IN_PROJ_ROWS = 256
IN_PROJ_SLABS = 4
FFN_ROWS = 128
NORM_ROWS = 256
VMEM_LIMIT_BYTES = 56 * 1024 * 1024


def _params(*semantics):
    return pltpu.CompilerParams(dimension_semantics=semantics, vmem_limit_bytes=VMEM_LIMIT_BYTES)


def _rms_scale(h, gain):
    return h * lax.rsqrt(jnp.mean(h * h, axis=-1, keepdims=True) + EPS) * gain


def _rms_rows_to(dst_ref, dst_row0, src_ref, gain, rows_per_pass=NORM_ROWS):
    n_rows = src_ref.shape[0]
    for r0 in range(0, n_rows, rows_per_pass):
        r1 = min(r0 + rows_per_pass, n_rows)
        dst_ref[dst_row0 + r0:dst_row0 + r1, :] = _rms_scale(src_ref[r0:r1, :], gain).astype(
            dst_ref.dtype)


def _rmsnorm_kernel(x_ref, g_ref, o_ref):
    o_ref[...] = _rms_scale(x_ref[...], g_ref[...]).astype(o_ref.dtype)


def _rmsnorm(x, gain, tm=512):
    m, d = x.shape
    return pl.pallas_call(
        _rmsnorm_kernel,
        grid=(m // tm,),
        in_specs=[pl.BlockSpec((tm, d), lambda i: (i, 0)), pl.BlockSpec((1, d), lambda i: (0, 0))],
        out_specs=pl.BlockSpec((tm, d), lambda i: (i, 0)),
        out_shape=jax.ShapeDtypeStruct((m, d), x.dtype),
        compiler_params=_params("parallel"),
        name="rmsnorm",
    )(x, gain.reshape(1, d))


def _rotate(x, cos, sin, head_dim):
    return x * cos + pltpu.roll(x, head_dim // 2, 1) * sin


def _in_proj_kernel(h_ref, g_ref, w_ref, cos_ref, sin_ref, o_ref, hn_ref, *scratch, head_dim,
                    rot_tiles, dilation):
    j = pl.program_id(1)

    @pl.when(j == 0)
    def _():
        _rms_rows_to(hn_ref, 0, h_ref, g_ref[...])

    tm = hn_ref.shape[0]
    tn = o_ref.shape[-1]
    chunk = max(MXU_COLS, head_dim)
    rb_rows = IN_PROJ_ROWS
    strided = dilation is not None and dilation > 1
    if strided:
        assert head_dim == LANES
        slab_ref, slab2_ref = scratch
        sub = rb_rows // dilation
        tile_sub = tm // dilation
        quarter = rb_rows // SINGLE_OP_STRIDE

    def tile(rotary):
        n_slabs_used = 0
        for c in range(tn // chunk):
            for rb in range(tm // rb_rows):
                rows = slice(rb * rb_rows, (rb + 1) * rb_rows)
                acc = jnp.dot(hn_ref[rows, :], w_ref[:, c * chunk:(c + 1) * chunk],
                              preferred_element_type=F32)
                for hd in range(chunk // head_dim):
                    cols = slice(c * chunk + hd * head_dim, c * chunk + (hd + 1) * head_dim)
                    x = acc[:, hd * head_dim:(hd + 1) * head_dim]
                    if not strided:
                        if rotary and dilation is None:
                            x = _rotate(x, cos_ref[0, rows, :], sin_ref[0, rows, :], head_dim)
                        elif rotary:
                            x = _rotate(x, cos_ref[0, 0, rows, :], sin_ref[0, 0, rows, :], head_dim)
                        if dilation is None:
                            o_ref[rows, cols] = x.astype(o_ref.dtype)
                        else:
                            o_ref[0, 0, rows, cols] = x.astype(o_ref.dtype)
                        continue
                    slot = n_slabs_used % slab_ref.shape[0]
                    n_slabs_used += 1
                    slab, slab2 = slab_ref.at[slot], slab2_ref.at[slot]
                    slab[...] = x
                    if dilation > SINGLE_OP_STRIDE:
                        assert dilation == SINGLE_OP_STRIDE * SINGLE_OP_STRIDE
                        for r1 in range(SINGLE_OP_STRIDE):
                            slab2[r1 * quarter:(r1 + 1) * quarter, :] = slab[
                                pl.ds(r1, quarter, stride=SINGLE_OP_STRIDE), :]
                    for r in range(dilation):
                        if dilation > SINGLE_OP_STRIDE:
                            r1, r2 = r % SINGLE_OP_STRIDE, r // SINGLE_OP_STRIDE
                            xr = slab2[pl.ds(r1 * quarter + r2, sub, stride=SINGLE_OP_STRIDE), :]
                        else:
                            xr = slab[pl.ds(r, sub, stride=dilation), :]
                        if rotary:
                            trows = slice(r * tile_sub + rb * sub, r * tile_sub + (rb + 1) * sub)
                            xr = _rotate(xr, cos_ref[0, 0, trows, :], sin_ref[0, 0, trows, :],
                                         head_dim)
                        o_ref[0, r, rb * sub:(rb + 1) * sub, cols] = xr.astype(o_ref.dtype)

    @pl.when(j < rot_tiles)
    def _():
        tile(rotary=True)

    @pl.when(j >= rot_tiles)
    def _():
        tile(rotary=False)


def _in_proj(h, gain, w, cos_tab, sin_tab, *, batch, seq, head_dim, n_out, col_offset=0,
             dilation=None, tm=1024, tn=1024, section=2048):
    m, k = h.shape
    seq_tiles = seq // tm
    col_tile0 = col_offset // tn
    rot_tiles = 2 * section // tn
    scratch = [pltpu.VMEM((tm, k), BF16)]
    if dilation is None:
        tab_spec = pl.BlockSpec((1, tm, head_dim),
                                lambda i, j: (jnp.minimum(j * tn // section, 1), i % seq_tiles, 0))
        out_spec = pl.BlockSpec((tm, tn), lambda i, j: (i, j))
        out_shape = jax.ShapeDtypeStruct((m, n_out), BF16)
    else:
        def residue_major(tab):
            t = tab.reshape(2, seq_tiles, tm // dilation, dilation, head_dim)
            return t.transpose(0, 1, 3, 2, 4).reshape(2, seq_tiles, tm, head_dim)

        cos_tab, sin_tab = residue_major(cos_tab), residue_major(sin_tab)
        tab_spec = pl.BlockSpec((1, 1, tm, head_dim),
                                lambda i, j: (jnp.minimum(j * tn // section, 1), i % seq_tiles, 0, 0))
        out_spec = pl.BlockSpec((1, dilation, tm // dilation, tn),
                                lambda i, j: (i // seq_tiles, 0, i % seq_tiles, j))
        out_shape = jax.ShapeDtypeStruct((batch, dilation, seq // dilation, n_out), BF16)
        if dilation > 1:
            scratch += [pltpu.VMEM((IN_PROJ_SLABS, IN_PROJ_ROWS, LANES), F32)] * 2
    return pl.pallas_call(
        functools.partial(_in_proj_kernel, head_dim=head_dim, rot_tiles=rot_tiles,
                          dilation=dilation),
        grid=(m // tm, n_out // tn),
        in_specs=[
            pl.BlockSpec((tm, k), lambda i, j: (i, 0)),
            pl.BlockSpec((1, k), lambda i, j: (0, 0)),
            pl.BlockSpec((k, tn), lambda i, j: (0, col_tile0 + j)),
            tab_spec,
            tab_spec,
        ],
        out_specs=out_spec,
        out_shape=out_shape,
        scratch_shapes=scratch,
        compiler_params=_params("parallel", "arbitrary"),
        name="in_proj" if dilation is None else f"in_proj_d{dilation}",
    )(h, gain.reshape(1, k), w, cos_tab, sin_tab)


A_SPAN = max(d for _, d in A_PATTERNS) * A_BLOCK
A_COMBINE_ROWS = 256


def _attn_kernel(*refs, heads):
    in_refs, out_ref, o_nat, lse_nat = refs[:-3], refs[-3], refs[-2], refs[-1]
    blk = A_BLOCK
    span = pl.program_id(1)
    qi = lax.broadcasted_iota(jnp.int32, (blk, 2 * blk), 0)
    kj = lax.broadcasted_iota(jnp.int32, (blk, 2 * blk), 1)

    for g, (window, dilation) in enumerate(A_PATTERNS):
        q_ref, kp_ref, kc_ref, vp_ref, vc_ref = in_refs[5 * g:5 * g + 5]
        n_back = window // dilation
        nq = A_SPAN // (dilation * blk)
        band = (kj >= qi + blk - n_back) & (kj <= qi + blk)
        band_first = (kj >= jnp.maximum(qi + blk - n_back, jnp.where(span > 0, 0, blk))) & (
            kj <= qi + blk)
        for r in range(dilation):
            for bq in range(nq):
                if dilation > 1:
                    rows = pl.ds(bq * blk * dilation + r, blk, stride=dilation)
                else:
                    rows = slice(bq * blk, (bq + 1) * blk)
                for h in range(heads):
                    hs = slice(h * A_HEAD_DIM, (h + 1) * A_HEAD_DIM)
                    q = q_ref[0, r, bq * blk:(bq + 1) * blk, hs]
                    if bq == 0:
                        k = jnp.concatenate([kp_ref[0, r, :, hs], kc_ref[0, r, 0:blk, hs]], axis=0)
                        v = jnp.concatenate([vp_ref[0, r, :, hs], vc_ref[0, r, 0:blk, hs]], axis=0)
                        valid = band_first
                    else:
                        k = kc_ref[0, r, (bq - 1) * blk:(bq + 1) * blk, hs]
                        v = vc_ref[0, r, (bq - 1) * blk:(bq + 1) * blk, hs]
                        valid = band
                    s = lax.dot_general(q, k, (((1,), (1,)), ((), ())),
                                        preferred_element_type=F32)
                    s = jnp.where(valid, s, -jnp.inf)
                    mx = jnp.max(s, axis=-1, keepdims=True)
                    p = jnp.exp(s - mx)
                    l = jnp.sum(p, axis=-1, keepdims=True)
                    o = jnp.dot(p.astype(BF16), v, preferred_element_type=F32)
                    o_nat[g * heads + h, rows, :] = o / l
                    lse_nat[g * heads + h, rows, :] = jnp.broadcast_to(mx + jnp.log(l), (blk, LANES))

    n_groups = len(A_PATTERNS)
    for h in range(heads):
        def combine(c, carry, h=h):
            rs = pl.ds(pl.multiple_of(c * A_COMBINE_ROWS, A_COMBINE_ROWS), A_COMBINE_ROWS)
            lses = [lse_nat[g * heads + h, rs, :] for g in range(n_groups)]
            mx = functools.reduce(jnp.maximum, lses)
            es = [jnp.exp(l - mx) for l in lses]
            num = sum(e * o_nat[g * heads + h, rs, :] for g, e in enumerate(es))
            out_ref[rs, h * A_HEAD_DIM:(h + 1) * A_HEAD_DIM] = (num / sum(es)).astype(out_ref.dtype)
            return carry
        lax.fori_loop(0, A_SPAN // A_COMBINE_ROWS, combine, 0)


def _attention(projs, *, batch, seq, heads=2):
    width = heads * A_HEAD_DIM
    sec_blocks = A_WIDTH // width
    n_spans = seq // A_SPAN
    in_specs = []
    for (window, dilation) in A_PATTERNS:
        sub_rows = A_SPAN // dilation
        nq = sub_rows // A_BLOCK

        def cur(t, dilation=dilation, sub_rows=sub_rows):
            return pl.BlockSpec((1, dilation, sub_rows, width),
                                lambda b, s, hh: (b, 0, s, t * sec_blocks + hh))

        def prev(t, dilation=dilation, nq=nq):
            return pl.BlockSpec((1, dilation, A_BLOCK, width),
                                lambda b, s, hh: (b, 0, jnp.maximum(s * nq - 1, 0),
                                                  t * sec_blocks + hh))

        in_specs += [cur(0), prev(1), cur(1), prev(2), cur(2)]
    n_slabs = len(A_PATTERNS) * heads
    return pl.pallas_call(
        functools.partial(_attn_kernel, heads=heads),
        grid=(batch, n_spans, sec_blocks),
        in_specs=in_specs,
        out_specs=pl.BlockSpec((A_SPAN, width), lambda b, s, hh: (b * n_spans + s, hh)),
        out_shape=jax.ShapeDtypeStruct((batch * seq, A_WIDTH), BF16),
        scratch_shapes=[pltpu.VMEM((n_slabs, A_SPAN, LANES), F32),
                        pltpu.VMEM((n_slabs, A_SPAN, LANES), F32)],
        compiler_params=_params("parallel", "parallel", "parallel"),
        name="attention",
    )(*[p for p in projs for _ in range(5)])


def _retention_kernel(q_ref, k_ref, v_ref, g_ref, decay_ref, qdec_ref, kdec_ref, cdec_ref,
                      o_ref, state_ref, *, chunks):
    c_len = R_CHUNK

    @pl.when(pl.program_id(2) == 0)
    def _():
        state_ref[...] = jnp.zeros_like(state_ref)

    decay = decay_ref[0]
    qdec = qdec_ref[0]
    kdec = kdec_ref[0]
    cdec = cdec_ref[0]
    for c in range(chunks):
        rows = slice(c * c_len, (c + 1) * c_len)
        q = q_ref[rows, :]
        k = k_ref[rows, :]
        v = v_ref[rows, :]
        scores = lax.dot_general(q, k, (((1,), (1,)), ((), ())), preferred_element_type=F32) * decay
        y = jnp.dot(scores.astype(BF16), v, preferred_element_type=F32)
        state = state_ref[...]
        y = y + jnp.dot(q, state.astype(BF16), preferred_element_type=F32) * qdec
        k_dec = (k.astype(F32) * kdec).astype(BF16)
        state_ref[...] = state * cdec + lax.dot_general(
            k_dec, v, (((0,), (0,)), ((), ())), preferred_element_type=F32)
        mu = jnp.mean(y, axis=-1, keepdims=True)
        yc = y - mu
        var = jnp.mean(yc * yc, axis=-1, keepdims=True)
        yn = yc * lax.rsqrt(var + EPS)
        g = g_ref[rows, :].astype(F32)
        o_ref[rows, :] = (g / (1.0 + jnp.exp(-g)) * yn).astype(o_ref.dtype)


def _retention_tables():
    heads = R_HEADS
    c_len = R_CHUNK
    log_gamma = jnp.log1p(-jnp.exp2(-5.0 - jnp.arange(heads, dtype=F32)))
    idx = jnp.arange(c_len, dtype=F32)
    rel = idx[:, None] - idx[None, :]
    decay = jnp.where(rel >= 0, jnp.exp(log_gamma[:, None, None] * jnp.maximum(rel, 0.0)), 0.0)
    q_decay = jnp.exp(log_gamma[None, :] * (idx[:, None] + 1.0))
    k_decay = jnp.exp(log_gamma[None, :] * (c_len - 1.0 - idx[:, None]))
    chunk_decay = jnp.exp(log_gamma * c_len)
    qdec = jnp.broadcast_to(q_decay.T[:, :, None], (heads, c_len, R_V_DIM))
    kdec = jnp.broadcast_to(k_decay.T[:, :, None], (heads, c_len, R_QK_DIM))
    cdec = jnp.broadcast_to(chunk_decay[:, None, None], (heads, 1, R_V_DIM))
    return decay, qdec, kdec, cdec


def _retention(proj, *, batch, seq, chunks=8):
    m = proj.shape[0]
    rows = chunks * R_CHUNK
    steps = seq // rows
    decay, qdec, kdec, cdec = _retention_tables()
    k_off = R_QK_WIDTH // R_QK_DIM
    v_off = 2 * R_QK_WIDTH // R_V_DIM
    g_off = (2 * R_QK_WIDTH + R_V_WIDTH) // R_V_DIM

    def tok(off):
        return lambda b, h, t: (b * steps + t, off + h)

    def per_head(b, h, t):
        return (h, 0, 0)

    return pl.pallas_call(
        functools.partial(_retention_kernel, chunks=chunks),
        grid=(batch, R_HEADS, steps),
        in_specs=[
            pl.BlockSpec((rows, R_QK_DIM), tok(0)),
            pl.BlockSpec((rows, R_QK_DIM), tok(k_off)),
            pl.BlockSpec((rows, R_V_DIM), tok(v_off)),
            pl.BlockSpec((rows, R_V_DIM), tok(g_off)),
            pl.BlockSpec((1, R_CHUNK, R_CHUNK), per_head),
            pl.BlockSpec((1, R_CHUNK, R_V_DIM), per_head),
            pl.BlockSpec((1, R_CHUNK, R_QK_DIM), per_head),
            pl.BlockSpec((1, 1, R_V_DIM), per_head),
        ],
        out_specs=pl.BlockSpec((rows, R_V_DIM), tok(0)),
        out_shape=jax.ShapeDtypeStruct((m, R_V_WIDTH), BF16),
        scratch_shapes=[pltpu.VMEM((R_QK_DIM, R_V_DIM), F32)],
        compiler_params=_params("parallel", "parallel", "arbitrary"),
        name="retention",
    )(proj, proj, proj, proj, decay, qdec, kdec, cdec)


def _ffn_up_kernel(h_ref, halo_ref, g_ref, wg_ref, wu_ref, cwg_ref, cwu_ref, cbg_ref, cbu_ref,
                   o_ref, hn_ref, *, tiles_per_seq):
    tm = h_ref.shape[0]
    pad = halo_ref.shape[0]
    tn = o_ref.shape[1]

    @pl.when(pl.program_id(1) == 0)
    def _():
        at_seq_start = pl.program_id(0) % tiles_per_seq == 0
        halo = _rms_scale(halo_ref[...], g_ref[...])
        hn_ref[0:pad, :] = jnp.where(at_seq_start, 0.0, halo).astype(hn_ref.dtype)
        _rms_rows_to(hn_ref, pad, h_ref, g_ref[...])

    def conv(window, cw, cb):
        y = cb + cw[CONV_WIDTH - 1:CONV_WIDTH] * window[F32_ROWS_PER_TILE:, :]
        for back in range(1, CONV_WIDTH):
            tap = CONV_WIDTH - 1 - back
            y = y + cw[tap:tap + 1] * window[F32_ROWS_PER_TILE - back:-back, :]
        return y

    for c in range(tn // MXU_COLS):
        cols = slice(c * MXU_COLS, (c + 1) * MXU_COLS)
        cwg, cwu, cbg, cbu = cwg_ref[:, cols], cwu_ref[:, cols], cbg_ref[:, cols], cbu_ref[:, cols]
        tail_g = tail_u = None
        for rb in range(tm // FFN_ROWS):
            row0 = 0 if rb == 0 else pad + rb * FFN_ROWS
            a = hn_ref[row0:pad + (rb + 1) * FFN_ROWS, :]
            ug = jnp.dot(a, wg_ref[:, cols], preferred_element_type=F32)
            uu = jnp.dot(a, wu_ref[:, cols], preferred_element_type=F32)
            if rb == 0:
                win_g = ug[pad - F32_ROWS_PER_TILE:, :]
                win_u = uu[pad - F32_ROWS_PER_TILE:, :]
            else:
                win_g = jnp.concatenate([tail_g, ug], axis=0)
                win_u = jnp.concatenate([tail_u, uu], axis=0)
            tail_g = win_g[FFN_ROWS:, :]
            tail_u = win_u[FFN_ROWS:, :]
            gate = conv(win_g, cwg, cbg)
            up = conv(win_u, cwu, cbu)
            o_ref[rb * FFN_ROWS:(rb + 1) * FFN_ROWS, cols] = (
                gate / (1.0 + jnp.exp(-gate)) * up).astype(o_ref.dtype)


def _ffn_up(h, gain, w_up, conv_w, conv_b, *, seq, tm=1024, tn=512):
    m, k = h.shape
    pad = BF16_ROWS_PER_TILE
    n_tiles = D_FF // tn
    halo_blocks = tm // pad

    def halo_map(i, j):
        return (jnp.maximum(i * halo_blocks - 1, 0), 0)

    conv_b = conv_b.reshape(1, 2 * D_FF)
    return pl.pallas_call(
        functools.partial(_ffn_up_kernel, tiles_per_seq=seq // tm),
        grid=(m // tm, n_tiles),
        in_specs=[
            pl.BlockSpec((tm, k), lambda i, j: (i, 0)),
            pl.BlockSpec((pad, k), halo_map),
            pl.BlockSpec((1, k), lambda i, j: (0, 0)),
            pl.BlockSpec((k, tn), lambda i, j: (0, j)),
            pl.BlockSpec((k, tn), lambda i, j: (0, n_tiles + j)),
            pl.BlockSpec((CONV_WIDTH, tn), lambda i, j: (0, j)),
            pl.BlockSpec((CONV_WIDTH, tn), lambda i, j: (0, n_tiles + j)),
            pl.BlockSpec((1, tn), lambda i, j: (0, j)),
            pl.BlockSpec((1, tn), lambda i, j: (0, n_tiles + j)),
        ],
        out_specs=pl.BlockSpec((tm, tn), lambda i, j: (i, j)),
        out_shape=jax.ShapeDtypeStruct((m, D_FF), BF16),
        scratch_shapes=[pltpu.VMEM((pad + tm, k), BF16)],
        compiler_params=_params("parallel", "arbitrary"),
        name="ffn_up",
    )(h, h, gain.reshape(1, k), w_up, w_up, conv_w, conv_w, conv_b, conv_b)


def _out_proj_kernel(a_ref, w_ref, h_ref, o_ref):
    for c in range(o_ref.shape[1] // MXU_COLS):
        cols = slice(c * MXU_COLS, (c + 1) * MXU_COLS)
        o_ref[:, cols] = h_ref[:, cols] + jnp.dot(a_ref[...], w_ref[:, cols],
                                                  preferred_element_type=F32)


def _out_proj(a, w, h, *, tm=1024, tn=512):
    m, k = a.shape
    d = w.shape[1]
    tile = pl.BlockSpec((tm, tn), lambda i, j: (i, j))
    return pl.pallas_call(
        _out_proj_kernel,
        grid=(m // tm, d // tn),
        in_specs=[
            pl.BlockSpec((tm, k), lambda i, j: (i, 0)),
            pl.BlockSpec((k, tn), lambda i, j: (0, j)),
            tile,
        ],
        out_specs=tile,
        out_shape=jax.ShapeDtypeStruct((m, d), F32),
        compiler_params=_params("parallel", "arbitrary"),
        name="out_proj",
    )(a, w, h)


def _rotary_tables(seq, inv_freq, q_scale, k_scale):
    ang = jnp.arange(seq, dtype=F32)[:, None] * inv_freq[None, :]
    cos, sin = jnp.cos(ang), jnp.sin(ang)
    cos_full = jnp.concatenate([cos, cos], axis=-1)
    sin_signed = jnp.concatenate([-sin, sin], axis=-1)
    scales = jnp.array([q_scale, k_scale], F32)[:, None, None]
    return cos_full[None] * scales, sin_signed[None] * scales


def kernel(x, norm_mix, norm_ffn, norm_final, w_in_attn, w_out_attn, w_in_ret, w_out_ret,
           w_up, conv_w, conv_b, w_down):
    batch, seq, d = x.shape
    depth = norm_mix.shape[0]
    inv_freq_a = ROPE_THETA ** (-jnp.arange(0, A_HEAD_DIM, 2, dtype=F32) / A_HEAD_DIM)
    cos_a, sin_a = _rotary_tables(seq, inv_freq_a, A_HEAD_DIM ** -0.5, 1.0)
    inv_freq_r = ROPE_THETA ** (-jnp.linspace(0.0, 1.0, R_QK_DIM // 2, dtype=F32))
    cos_r, sin_r = _rotary_tables(seq, inv_freq_r, 1.0, R_QK_DIM ** -0.5)

    h = x.reshape(batch * seq, d)
    for i in range(depth):
        li = i // 2
        if i % 2 == 0:
            w_in = w_in_attn[li].astype(BF16)
            group_cols = w_in.shape[1] // A_N_GROUPS
            projs = [
                _in_proj(h, norm_mix[i], w_in, cos_a, sin_a, batch=batch, seq=seq,
                         head_dim=A_HEAD_DIM, n_out=group_cols, col_offset=g * group_cols,
                         dilation=dilation)
                for g, (_, dilation) in enumerate(A_PATTERNS)
            ]
            mixed = _attention(projs, batch=batch, seq=seq)
            h = _out_proj(mixed, w_out_attn[li].astype(BF16), h, tn=1024)
        else:
            w_in = w_in_ret[li].astype(BF16)
            proj = _in_proj(h, norm_mix[i], w_in, cos_r, sin_r, batch=batch, seq=seq,
                            head_dim=R_QK_DIM, n_out=w_in.shape[1])
            mixed = _retention(proj, batch=batch, seq=seq)
            h = _out_proj(mixed, w_out_ret[li].astype(BF16), h)
        act = _ffn_up(h, norm_ffn[i], w_up[i].astype(BF16), conv_w[i], conv_b[i], seq=seq)
        h = _out_proj(act, w_down[i].astype(BF16), h)
    return _rmsnorm(h, norm_final).reshape(batch, seq, d)
```

```python
import functools

import jax
import jax.numpy as jnp
from jax import lax
from jax.experimental import pallas as pl
from jax.experimental.pallas import tpu as pltpu

F32 = jnp.float32
BF16 = jnp.bfloat16

D_MODEL = 2048
EPS = 1e-6
ROPE_THETA = 10000.0

A_HEAD_DIM = 128
A_HEADS = D_MODEL // A_HEAD_DIM
A_WIDTH = A_HEADS * A_HEAD_DIM
A_PATTERNS = ((128, 1), (512, 4), (2048, 16))
A_N_GROUPS = len(A_PATTERNS)
A_BLOCK = 128

R_QK_DIM = 256
R_V_DIM = 512
R_HEADS = D_MODEL // R_QK_DIM
R_QK_WIDTH = R_HEADS * R_QK_DIM
R_V_WIDTH = R_HEADS * R_V_DIM
R_CHUNK = 128

D_FF = 5632
CONV_WIDTH = 3

BF16_ROWS_PER_TILE = 16
LANES = 128
MXU_COLS = 256
SINGLE_OP_STRIDE = 4
IN_PROJ_ROWS = 256
IN_PROJ_SLABS = 4
NORM_ROWS = 256
VMEM_LIMIT_BYTES = 56 * 1024 * 1024


def _params(*semantics):
    return pltpu.CompilerParams(dimension_semantics=semantics, vmem_limit_bytes=VMEM_LIMIT_BYTES)


def _rms_scale(h, gain):
    return h * lax.rsqrt(jnp.mean(h * h, axis=-1, keepdims=True) + EPS) * gain


def _rms_rows_to(dst_ref, dst_row0, src_ref, gain, rows_per_pass=NORM_ROWS):
    n_rows = src_ref.shape[0]
    for r0 in range(0, n_rows, rows_per_pass):
        r1 = min(r0 + rows_per_pass, n_rows)
        dst_ref[dst_row0 + r0:dst_row0 + r1, :] = _rms_scale(src_ref[r0:r1, :], gain).astype(
            dst_ref.dtype)


def _rmsnorm_kernel(x_ref, g_ref, o_ref):
    o_ref[...] = _rms_scale(x_ref[...], g_ref[...]).astype(o_ref.dtype)


def _rmsnorm(x, gain, tm=512):
    m, d = x.shape
    return pl.pallas_call(
        _rmsnorm_kernel,
        grid=(m // tm,),
        in_specs=[pl.BlockSpec((tm, d), lambda i: (i, 0)), pl.BlockSpec((1, d), lambda i: (0, 0))],
        out_specs=pl.BlockSpec((tm, d), lambda i: (i, 0)),
        out_shape=jax.ShapeDtypeStruct((m, d), x.dtype),
        compiler_params=_params("parallel"),
        name="rmsnorm",
    )(x, gain.reshape(1, d))


def _rotate(x, cos, sin, head_dim):
    return x * cos + pltpu.roll(x, head_dim // 2, 1) * sin


def _in_proj_kernel(h_ref, g_ref, w_ref, cos_ref, sin_ref, o_ref, hn_ref, *scratch, head_dim,
                    rot_tiles, dilation):
    j = pl.program_id(1)

    @pl.when(j == 0)
    def _():
        _rms_rows_to(hn_ref, 0, h_ref, g_ref[...])

    tm = hn_ref.shape[0]
    tn = o_ref.shape[-1]
    chunk = max(MXU_COLS, head_dim)
    rb_rows = IN_PROJ_ROWS
    strided = dilation is not None and dilation > 1
    if strided:
        assert head_dim == LANES
        slab_ref, slab2_ref = scratch
        sub = rb_rows // dilation
        tile_sub = tm // dilation
        quarter = rb_rows // SINGLE_OP_STRIDE

    def tile(rotary):
        n_slabs_used = 0
        for c in range(tn // chunk):
            for rb in range(tm // rb_rows):
                rows = slice(rb * rb_rows, (rb + 1) * rb_rows)
                acc = jnp.dot(hn_ref[rows, :], w_ref[:, c * chunk:(c + 1) * chunk],
                              preferred_element_type=F32)
                for hd in range(chunk // head_dim):
                    cols = slice(c * chunk + hd * head_dim, c * chunk + (hd + 1) * head_dim)
                    x = acc[:, hd * head_dim:(hd + 1) * head_dim]
                    if not strided:
                        if rotary and dilation is None:
                            x = _rotate(x, cos_ref[0, rows, :], sin_ref[0, rows, :], head_dim)
                        elif rotary:
                            x = _rotate(x, cos_ref[0, 0, rows, :], sin_ref[0, 0, rows, :], head_dim)
                        if dilation is None:
                            o_ref[rows, cols] = x.astype(o_ref.dtype)
                        else:
                            o_ref[0, 0, rows, cols] = x.astype(o_ref.dtype)
                        continue
                    slot = n_slabs_used % slab_ref.shape[0]
                    n_slabs_used += 1
                    slab, slab2 = slab_ref.at[slot], slab2_ref.at[slot]
                    slab[...] = x
                    if dilation > SINGLE_OP_STRIDE:
                        assert dilation == SINGLE_OP_STRIDE * SINGLE_OP_STRIDE
                        for r1 in range(SINGLE_OP_STRIDE):
                            slab2[r1 * quarter:(r1 + 1) * quarter, :] = slab[
                                pl.ds(r1, quarter, stride=SINGLE_OP_STRIDE), :]
                    for r in range(dilation):
                        if dilation > SINGLE_OP_STRIDE:
                            r1, r2 = r % SINGLE_OP_STRIDE, r // SINGLE_OP_STRIDE
                            xr = slab2[pl.ds(r1 * quarter + r2, sub, stride=SINGLE_OP_STRIDE), :]
                        else:
                            xr = slab[pl.ds(r, sub, stride=dilation), :]
                        if rotary:
                            trows = slice(r * tile_sub + rb * sub, r * tile_sub + (rb + 1) * sub)
                            xr = _rotate(xr, cos_ref[0, 0, trows, :], sin_ref[0, 0, trows, :],
                                         head_dim)
                        o_ref[0, r, rb * sub:(rb + 1) * sub, cols] = xr.astype(o_ref.dtype)

    @pl.when(j < rot_tiles)
    def _():
        tile(rotary=True)

    @pl.when(j >= rot_tiles)
    def _():
        tile(rotary=False)


def _in_proj(h, gain, w, cos_tab, sin_tab, *, batch, seq, head_dim, n_out, col_offset=0,
             dilation=None, tm=1024, tn=1024, section=2048):
    m, k = h.shape
    seq_tiles = seq // tm
    col_tile0 = col_offset // tn
    rot_tiles = 2 * section // tn
    scratch = [pltpu.VMEM((tm, k), BF16)]
    if dilation is None:
        tab_spec = pl.BlockSpec((1, tm, head_dim),
                                lambda i, j: (jnp.minimum(j * tn // section, 1), i % seq_tiles, 0))
        out_spec = pl.BlockSpec((tm, tn), lambda i, j: (i, j))
        out_shape = jax.ShapeDtypeStruct((m, n_out), BF16)
    else:
        def residue_major(tab):
            t = tab.reshape(2, seq_tiles, tm // dilation, dilation, head_dim)
            return t.transpose(0, 1, 3, 2, 4).reshape(2, seq_tiles, tm, head_dim)

        cos_tab, sin_tab = residue_major(cos_tab), residue_major(sin_tab)
        tab_spec = pl.BlockSpec((1, 1, tm, head_dim),
                                lambda i, j: (jnp.minimum(j * tn // section, 1), i % seq_tiles, 0, 0))
        out_spec = pl.BlockSpec((1, dilation, tm // dilation, tn),
                                lambda i, j: (i // seq_tiles, 0, i % seq_tiles, j))
        out_shape = jax.ShapeDtypeStruct((batch, dilation, seq // dilation, n_out), BF16)
        if dilation > 1:
            scratch += [pltpu.VMEM((IN_PROJ_SLABS, IN_PROJ_ROWS, LANES), F32)] * 2
    return pl.pallas_call(
        functools.partial(_in_proj_kernel, head_dim=head_dim, rot_tiles=rot_tiles,
                          dilation=dilation),
        grid=(m // tm, n_out // tn),
        in_specs=[
            pl.BlockSpec((tm, k), lambda i, j: (i, 0)),
            pl.BlockSpec((1, k), lambda i, j: (0, 0)),
            pl.BlockSpec((k, tn), lambda i, j: (0, col_tile0 + j)),
            tab_spec,
            tab_spec,
        ],
        out_specs=out_spec,
        out_shape=out_shape,
        scratch_shapes=scratch,
        compiler_params=_params("parallel", "arbitrary"),
        name="in_proj" if dilation is None else f"in_proj_d{dilation}",
    )(h, gain.reshape(1, k), w, cos_tab, sin_tab)


A_SPAN = max(d for _, d in A_PATTERNS) * A_BLOCK
A_COMBINE_ROWS = 256


def _attn_kernel(*refs, heads):
    in_refs, out_ref, o_nat, lse_nat = refs[:-3], refs[-3], refs[-2], refs[-1]
    blk = A_BLOCK
    span = pl.program_id(1)
    qi = lax.broadcasted_iota(jnp.int32, (blk, 2 * blk), 0)
    kj = lax.broadcasted_iota(jnp.int32, (blk, 2 * blk), 1)

    for g, (window, dilation) in enumerate(A_PATTERNS):
        q_ref, kp_ref, kc_ref, vp_ref, vc_ref = in_refs[5 * g:5 * g + 5]
        n_back = window // dilation
        nq = A_SPAN // (dilation * blk)
        band = (kj >= qi + blk - n_back) & (kj <= qi + blk)
        band_first = (kj >= jnp.maximum(qi + blk - n_back, jnp.where(span > 0, 0, blk))) & (
            kj <= qi + blk)
        for r in range(dilation):
            for bq in range(nq):
                if dilation > 1:
                    rows = pl.ds(bq * blk * dilation + r, blk, stride=dilation)
                else:
                    rows = slice(bq * blk, (bq + 1) * blk)
                for h in range(heads):
                    hs = slice(h * A_HEAD_DIM, (h + 1) * A_HEAD_DIM)
                    q = q_ref[0, r, bq * blk:(bq + 1) * blk, hs]
                    if bq == 0:
                        k = jnp.concatenate([kp_ref[0, r, :, hs], kc_ref[0, r, 0:blk, hs]], axis=0)
                        v = jnp.concatenate([vp_ref[0, r, :, hs], vc_ref[0, r, 0:blk, hs]], axis=0)
                        valid = band_first
                    else:
                        k = kc_ref[0, r, (bq - 1) * blk:(bq + 1) * blk, hs]
                        v = vc_ref[0, r, (bq - 1) * blk:(bq + 1) * blk, hs]
                        valid = band
                    s = lax.dot_general(q, k, (((1,), (1,)), ((), ())),
                                        preferred_element_type=F32)
                    s = jnp.where(valid, s, -jnp.inf)
                    mx = jnp.max(s, axis=-1, keepdims=True)
                    p = jnp.exp(s - mx)
                    l = jnp.sum(p, axis=-1, keepdims=True)
                    o = jnp.dot(p.astype(BF16), v, preferred_element_type=F32)
                    o_nat[g * heads + h, rows, :] = o / l
                    lse_nat[g * heads + h, rows, :] = jnp.broadcast_to(mx + jnp.log(l), (blk, LANES))

    n_groups = len(A_PATTERNS)
    for h in range(heads):
        def combine(c, carry, h=h):
            rs = pl.ds(pl.multiple_of(c * A_COMBINE_ROWS, A_COMBINE_ROWS), A_COMBINE_ROWS)
            lses = [lse_nat[g * heads + h, rs, :] for g in range(n_groups)]
            mx = functools.reduce(jnp.maximum, lses)
            es = [jnp.exp(l - mx) for l in lses]
            num = sum(e * o_nat[g * heads + h, rs, :] for g, e in enumerate(es))
            out_ref[rs, h * A_HEAD_DIM:(h + 1) * A_HEAD_DIM] = (num / sum(es)).astype(out_ref.dtype)
            return carry
        lax.fori_loop(0, A_SPAN // A_COMBINE_ROWS, combine, 0)


def _attention(projs, *, batch, seq, heads=2):
    width = heads * A_HEAD_DIM
    sec_blocks = A_WIDTH // width
    n_spans = seq // A_SPAN
    in_specs = []
    for (window, dilation) in A_PATTERNS:
        sub_rows = A_SPAN // dilation
        nq = sub_rows // A_BLOCK

        def cur(t, dilation=dilation, sub_rows=sub_rows):
            return pl.BlockSpec((1, dilation, sub_rows, width),
                                lambda b, s, hh: (b, 0, s, t * sec_blocks + hh))

        def prev(t, dilation=dilation, nq=nq):
            return pl.BlockSpec((1, dilation, A_BLOCK, width),
                                lambda b, s, hh: (b, 0, jnp.maximum(s * nq - 1, 0),
                                                  t * sec_blocks + hh))

        in_specs += [cur(0), prev(1), cur(1), prev(2), cur(2)]
    n_slabs = len(A_PATTERNS) * heads
    return pl.pallas_call(
        functools.partial(_attn_kernel, heads=heads),
        grid=(batch, n_spans, sec_blocks),
        in_specs=in_specs,
        out_specs=pl.BlockSpec((A_SPAN, width), lambda b, s, hh: (b * n_spans + s, hh)),
        out_shape=jax.ShapeDtypeStruct((batch * seq, A_WIDTH), BF16),
        scratch_shapes=[pltpu.VMEM((n_slabs, A_SPAN, LANES), F32),
                        pltpu.VMEM((n_slabs, A_SPAN, LANES), F32)],
        compiler_params=_params("parallel", "parallel", "parallel"),
        name="attention",
    )(*[p for p in projs for _ in range(5)])


def _retention_kernel(q_ref, k_ref, v_ref, g_ref, decay_ref, qdec_ref, kdec_ref, cdec_ref,
                      o_ref, state_ref, *, chunks):
    c_len = R_CHUNK

    @pl.when(pl.program_id(2) == 0)
    def _():
        state_ref[...] = jnp.zeros_like(state_ref)

    decay = decay_ref[0]
    qdec = qdec_ref[0]
    kdec = kdec_ref[0]
    cdec = cdec_ref[0]
    for c in range(chunks):
        rows = slice(c * c_len, (c + 1) * c_len)
        q = q_ref[rows, :]
        k = k_ref[rows, :]
        v = v_ref[rows, :]
        scores = lax.dot_general(q, k, (((1,), (1,)), ((), ())), preferred_element_type=F32) * decay
        y = jnp.dot(scores.astype(BF16), v, preferred_element_type=F32)
        state = state_ref[...]
        y = y + jnp.dot(q, state.astype(BF16), preferred_element_type=F32) * qdec
        k_dec = (k.astype(F32) * kdec).astype(BF16)
        state_ref[...] = state * cdec + lax.dot_general(
            k_dec, v, (((0,), (0,)), ((), ())), preferred_element_type=F32)
        mu = jnp.mean(y, axis=-1, keepdims=True)
        yc = y - mu
        var = jnp.mean(yc * yc, axis=-1, keepdims=True)
        yn = yc * lax.rsqrt(var + EPS)
        g = g_ref[rows, :].astype(F32)
        o_ref[rows, :] = (g / (1.0 + jnp.exp(-g)) * yn).astype(o_ref.dtype)


def _retention_tables():
    heads = R_HEADS
    c_len = R_CHUNK
    log_gamma = jnp.log1p(-jnp.exp2(-5.0 - jnp.arange(heads, dtype=F32)))
    idx = jnp.arange(c_len, dtype=F32)
    rel = idx[:, None] - idx[None, :]
    decay = jnp.where(rel >= 0, jnp.exp(log_gamma[:, None, None] * jnp.maximum(rel, 0.0)), 0.0)
    q_decay = jnp.exp(log_gamma[None, :] * (idx[:, None] + 1.0))
    k_decay = jnp.exp(log_gamma[None, :] * (c_len - 1.0 - idx[:, None]))
    chunk_decay = jnp.exp(log_gamma * c_len)
    qdec = jnp.broadcast_to(q_decay.T[:, :, None], (heads, c_len, R_V_DIM))
    kdec = jnp.broadcast_to(k_decay.T[:, :, None], (heads, c_len, R_QK_DIM))
    cdec = jnp.broadcast_to(chunk_decay[:, None, None], (heads, 1, R_V_DIM))
    return decay, qdec, kdec, cdec


def _retention(proj, *, batch, seq, chunks=8):
    m = proj.shape[0]
    rows = chunks * R_CHUNK
    steps = seq // rows
    decay, qdec, kdec, cdec = _retention_tables()
    k_off = R_QK_WIDTH // R_QK_DIM
    v_off = 2 * R_QK_WIDTH // R_V_DIM
    g_off = (2 * R_QK_WIDTH + R_V_WIDTH) // R_V_DIM

    def tok(off):
        return lambda b, h, t: (b * steps + t, off + h)

    def per_head(b, h, t):
        return (h, 0, 0)

    return pl.pallas_call(
        functools.partial(_retention_kernel, chunks=chunks),
        grid=(batch, R_HEADS, steps),
        in_specs=[
            pl.BlockSpec((rows, R_QK_DIM), tok(0)),
            pl.BlockSpec((rows, R_QK_DIM), tok(k_off)),
            pl.BlockSpec((rows, R_V_DIM), tok(v_off)),
            pl.BlockSpec((rows, R_V_DIM), tok(g_off)),
            pl.BlockSpec((1, R_CHUNK, R_CHUNK), per_head),
            pl.BlockSpec((1, R_CHUNK, R_V_DIM), per_head),
            pl.BlockSpec((1, R_CHUNK, R_QK_DIM), per_head),
            pl.BlockSpec((1, 1, R_V_DIM), per_head),
        ],
        out_specs=pl.BlockSpec((rows, R_V_DIM), tok(0)),
        out_shape=jax.ShapeDtypeStruct((m, R_V_WIDTH), BF16),
        scratch_shapes=[pltpu.VMEM((R_QK_DIM, R_V_DIM), F32)],
        compiler_params=_params("parallel", "parallel", "arbitrary"),
        name="retention",
    )(proj, proj, proj, proj, decay, qdec, kdec, cdec)


def _ffn_up_kernel(h_ref, halo_ref, g_ref, wg_ref, wu_ref, cwg_ref, cwu_ref, cbg_ref, cbu_ref,
                   o_ref, hn_ref, ug_ref, uu_ref, *, tiles_per_seq):
    tm = h_ref.shape[0]
    pad = halo_ref.shape[0]
    tn = o_ref.shape[1]

    @pl.when(pl.program_id(1) == 0)
    def _():
        at_seq_start = pl.program_id(0) % tiles_per_seq == 0
        halo = _rms_scale(halo_ref[...], g_ref[...])
        hn_ref[0:pad, :] = jnp.where(at_seq_start, 0.0, halo).astype(hn_ref.dtype)
        _rms_rows_to(hn_ref, pad, h_ref, g_ref[...])

    for c in range(tn // MXU_COLS):
        cols = slice(c * MXU_COLS, (c + 1) * MXU_COLS)
        slot = c % ug_ref.shape[0]

        def conv(u_ref, w_ref, cw_ref, cb_ref):
            u = u_ref.at[slot]
            u[...] = jnp.dot(hn_ref[...], w_ref[:, cols], preferred_element_type=F32)
            cw = cw_ref[:, cols]
            y = cb_ref[:, cols] + cw[CONV_WIDTH - 1:CONV_WIDTH] * u[pad:pad + tm, :]
            for back in range(1, CONV_WIDTH):
                tap = CONV_WIDTH - 1 - back
                y = y + cw[tap:tap + 1] * u[pad - back:pad - back + tm, :]
            return y

        gate = conv(ug_ref, wg_ref, cwg_ref, cbg_ref)
        up = conv(uu_ref, wu_ref, cwu_ref, cbu_ref)
        o_ref[:, cols] = (gate / (1.0 + jnp.exp(-gate)) * up).astype(o_ref.dtype)


def _ffn_up(h, gain, w_up, conv_w, conv_b, *, seq, tm=1024, tn=512):
    m, k = h.shape
    pad = BF16_ROWS_PER_TILE
    n_tiles = D_FF // tn
    halo_blocks = tm // pad

    def halo_map(i, j):
        return (jnp.maximum(i * halo_blocks - 1, 0), 0)

    conv_b = conv_b.reshape(1, 2 * D_FF)
    return pl.pallas_call(
        functools.partial(_ffn_up_kernel, tiles_per_seq=seq // tm),
        grid=(m // tm, n_tiles),
        in_specs=[
            pl.BlockSpec((tm, k), lambda i, j: (i, 0)),
            pl.BlockSpec((pad, k), halo_map),
            pl.BlockSpec((1, k), lambda i, j: (0, 0)),
            pl.BlockSpec((k, tn), lambda i, j: (0, j)),
            pl.BlockSpec((k, tn), lambda i, j: (0, n_tiles + j)),
            pl.BlockSpec((CONV_WIDTH, tn), lambda i, j: (0, j)),
            pl.BlockSpec((CONV_WIDTH, tn), lambda i, j: (0, n_tiles + j)),
            pl.BlockSpec((1, tn), lambda i, j: (0, j)),
            pl.BlockSpec((1, tn), lambda i, j: (0, n_tiles + j)),
        ],
        out_specs=pl.BlockSpec((tm, tn), lambda i, j: (i, j)),
        out_shape=jax.ShapeDtypeStruct((m, D_FF), BF16),
        scratch_shapes=[pltpu.VMEM((pad + tm, k), BF16),
                        pltpu.VMEM((2, pad + tm, MXU_COLS), F32),
                        pltpu.VMEM((2, pad + tm, MXU_COLS), F32)],
        compiler_params=_params("parallel", "arbitrary"),
        name="ffn_up",
    )(h, h, gain.reshape(1, k), w_up, w_up, conv_w, conv_w, conv_b, conv_b)


def _out_proj_kernel(a_ref, w_ref, h_ref, o_ref):
    for c in range(o_ref.shape[1] // MXU_COLS):
        cols = slice(c * MXU_COLS, (c + 1) * MXU_COLS)
        o_ref[:, cols] = h_ref[:, cols] + jnp.dot(a_ref[...], w_ref[:, cols],
                                                  preferred_element_type=F32)


def _out_proj(a, w, h, *, tm=1024, tn=512):
    m, k = a.shape
    d = w.shape[1]
    tile = pl.BlockSpec((tm, tn), lambda i, j: (i, j))
    return pl.pallas_call(
        _out_proj_kernel,
        grid=(m // tm, d // tn),
        in_specs=[
            pl.BlockSpec((tm, k), lambda i, j: (i, 0)),
            pl.BlockSpec((k, tn), lambda i, j: (0, j)),
            tile,
        ],
        out_specs=tile,
        out_shape=jax.ShapeDtypeStruct((m, d), F32),
        compiler_params=_params("parallel", "arbitrary"),
        name="out_proj",
    )(a, w, h)


def _rotary_tables(seq, inv_freq, q_scale, k_scale):
    ang = jnp.arange(seq, dtype=F32)[:, None] * inv_freq[None, :]
    cos, sin = jnp.cos(ang), jnp.sin(ang)
    cos_full = jnp.concatenate([cos, cos], axis=-1)
    sin_signed = jnp.concatenate([-sin, sin], axis=-1)
    scales = jnp.array([q_scale, k_scale], F32)[:, None, None]
    return cos_full[None] * scales, sin_signed[None] * scales


def kernel(x, norm_mix, norm_ffn, norm_final, w_in_attn, w_out_attn, w_in_ret, w_out_ret,
           w_up, conv_w, conv_b, w_down):
    batch, seq, d = x.shape
    depth = norm_mix.shape[0]
    inv_freq_a = ROPE_THETA ** (-jnp.arange(0, A_HEAD_DIM, 2, dtype=F32) / A_HEAD_DIM)
    cos_a, sin_a = _rotary_tables(seq, inv_freq_a, A_HEAD_DIM ** -0.5, 1.0)
    inv_freq_r = ROPE_THETA ** (-jnp.linspace(0.0, 1.0, R_QK_DIM // 2, dtype=F32))
    cos_r, sin_r = _rotary_tables(seq, inv_freq_r, 1.0, R_QK_DIM ** -0.5)

    h = x.reshape(batch * seq, d)
    for i in range(depth):
        li = i // 2
        if i % 2 == 0:
            w_in = w_in_attn[li].astype(BF16)
            group_cols = w_in.shape[1] // A_N_GROUPS
            projs = [
                _in_proj(h, norm_mix[i], w_in, cos_a, sin_a, batch=batch, seq=seq,
                         head_dim=A_HEAD_DIM, n_out=group_cols, col_offset=g * group_cols,
                         dilation=dilation)
                for g, (_, dilation) in enumerate(A_PATTERNS)
            ]
            mixed = _attention(projs, batch=batch, seq=seq)
            h = _out_proj(mixed, w_out_attn[li].astype(BF16), h, tn=1024)
        else:
            w_in = w_in_ret[li].astype(BF16)
            proj = _in_proj(h, norm_mix[i], w_in, cos_r, sin_r, batch=batch, seq=seq,
                            head_dim=R_QK_DIM, n_out=w_in.shape[1])
            mixed = _retention(proj, batch=batch, seq=seq)
            h = _out_proj(mixed, w_out_ret[li].astype(BF16), h)
        act = _ffn_up(h, norm_ffn[i], w_up[i].astype(BF16), conv_w[i], conv_b[i], seq=seq)
        h = _out_proj(act, w_down[i].astype(BF16), h)
    return _rmsnorm(h, norm_final).reshape(batch, seq, d)
```

```python
import functools

import jax
import jax.numpy as jnp
from jax import lax
from jax.experimental import pallas as pl
from jax.experimental.pallas import tpu as pltpu

F32 = jnp.float32
BF16 = jnp.bfloat16

D_MODEL = 2048
EPS = 1e-6
ROPE_THETA = 10000.0

A_HEAD_DIM = 128
A_HEADS = D_MODEL // A_HEAD_DIM
A_WIDTH = A_HEADS * A_HEAD_DIM
A_PATTERNS = ((128, 1), (512, 4), (2048, 16))
A_N_GROUPS = len(A_PATTERNS)
A_BLOCK = 128

R_QK_DIM = 256
R_V_DIM = 512
R_HEADS = D_MODEL // R_QK_DIM
R_QK_WIDTH = R_HEADS * R_QK_DIM
R_V_WIDTH = R_HEADS * R_V_DIM
R_CHUNK = 128

D_FF = 5632
CONV_WIDTH = 3

BF16_ROWS_PER_TILE = 16
LANES = 128
MXU_COLS = 256
SINGLE_OP_STRIDE = 4
IN_PROJ_ROWS = 256
IN_PROJ_SLABS = 4
NORM_ROWS = 256
VMEM_LIMIT_BYTES = 56 * 1024 * 1024


def _params(*semantics):
    return pltpu.CompilerParams(dimension_semantics=semantics, vmem_limit_bytes=VMEM_LIMIT_BYTES)


def _rms_scale(h, gain):
    return h * lax.rsqrt(jnp.mean(h * h, axis=-1, keepdims=True) + EPS) * gain


def _rms_rows_to(dst_ref, dst_row0, src_ref, gain, rows_per_pass=NORM_ROWS):
    n_rows = src_ref.shape[0]
    for r0 in range(0, n_rows, rows_per_pass):
        r1 = min(r0 + rows_per_pass, n_rows)
        dst_ref[dst_row0 + r0:dst_row0 + r1, :] = _rms_scale(src_ref[r0:r1, :], gain).astype(
            dst_ref.dtype)


def _rmsnorm_kernel(x_ref, g_ref, o_ref):
    o_ref[...] = _rms_scale(x_ref[...], g_ref[...]).astype(o_ref.dtype)


def _rmsnorm(x, gain, tm=512):
    m, d = x.shape
    return pl.pallas_call(
        _rmsnorm_kernel,
        grid=(m // tm,),
        in_specs=[pl.BlockSpec((tm, d), lambda i: (i, 0)), pl.BlockSpec((1, d), lambda i: (0, 0))],
        out_specs=pl.BlockSpec((tm, d), lambda i: (i, 0)),
        out_shape=jax.ShapeDtypeStruct((m, d), x.dtype),
        compiler_params=_params("parallel"),
        name="rmsnorm",
    )(x, gain.reshape(1, d))


def _rotate(x, cos, sin, head_dim):
    return x * cos + pltpu.roll(x, head_dim // 2, 1) * sin


def _in_proj_kernel(h_ref, g_ref, w_ref, cos_ref, sin_ref, o_ref, hn_ref, *scratch, head_dim,
                    rot_tiles, dilation):
    j = pl.program_id(1)

    @pl.when(j == 0)
    def _():
        _rms_rows_to(hn_ref, 0, h_ref, g_ref[...])

    tm = hn_ref.shape[0]
    tn = o_ref.shape[-1]
    chunk = max(MXU_COLS, head_dim)
    rb_rows = IN_PROJ_ROWS
    strided = dilation is not None and dilation > 1
    if strided:
        assert head_dim == LANES
        slab_ref, slab2_ref = scratch
        sub = rb_rows // dilation
        tile_sub = tm // dilation
        quarter = rb_rows // SINGLE_OP_STRIDE

    def tile(rotary):
        n_slabs_used = 0
        for c in range(tn // chunk):
            for rb in range(tm // rb_rows):
                rows = slice(rb * rb_rows, (rb + 1) * rb_rows)
                acc = jnp.dot(hn_ref[rows, :], w_ref[:, c * chunk:(c + 1) * chunk],
                              preferred_element_type=F32)
                for hd in range(chunk // head_dim):
                    cols = slice(c * chunk + hd * head_dim, c * chunk + (hd + 1) * head_dim)
                    x = acc[:, hd * head_dim:(hd + 1) * head_dim]
                    if not strided:
                        if rotary and dilation is None:
                            x = _rotate(x, cos_ref[0, rows, :], sin_ref[0, rows, :], head_dim)
                        elif rotary:
                            x = _rotate(x, cos_ref[0, 0, rows, :], sin_ref[0, 0, rows, :], head_dim)
                        if dilation is None:
                            o_ref[rows, cols] = x.astype(o_ref.dtype)
                        else:
                            o_ref[0, 0, rows, cols] = x.astype(o_ref.dtype)
                        continue
                    slot = n_slabs_used % slab_ref.shape[0]
                    n_slabs_used += 1
                    slab, slab2 = slab_ref.at[slot], slab2_ref.at[slot]
                    slab[...] = x
                    if dilation > SINGLE_OP_STRIDE:
                        assert dilation == SINGLE_OP_STRIDE * SINGLE_OP_STRIDE
                        for r1 in range(SINGLE_OP_STRIDE):
                            slab2[r1 * quarter:(r1 + 1) * quarter, :] = slab[
                                pl.ds(r1, quarter, stride=SINGLE_OP_STRIDE), :]
                    for r in range(dilation):
                        if dilation > SINGLE_OP_STRIDE:
                            r1, r2 = r % SINGLE_OP_STRIDE, r // SINGLE_OP_STRIDE
                            xr = slab2[pl.ds(r1 * quarter + r2, sub, stride=SINGLE_OP_STRIDE), :]
                        else:
                            xr = slab[pl.ds(r, sub, stride=dilation), :]
                        if rotary:
                            trows = slice(r * tile_sub + rb * sub, r * tile_sub + (rb + 1) * sub)
                            xr = _rotate(xr, cos_ref[0, 0, trows, :], sin_ref[0, 0, trows, :],
                                         head_dim)
                        o_ref[0, r, rb * sub:(rb + 1) * sub, cols] = xr.astype(o_ref.dtype)

    @pl.when(j < rot_tiles)
    def _():
        tile(rotary=True)

    @pl.when(j >= rot_tiles)
    def _():
        tile(rotary=False)


def _in_proj(h, gain, w, cos_tab, sin_tab, *, layer, batch, seq, head_dim, n_out, col_offset=0,
             dilation=None, tm=1024, tn=1024, section=2048):
    m, k = h.shape
    seq_tiles = seq // tm
    col_tile0 = col_offset // tn
    rot_tiles = 2 * section // tn
    scratch = [pltpu.VMEM((tm, k), BF16)]
    if dilation is None:
        tab_spec = pl.BlockSpec((1, tm, head_dim),
                                lambda i, j: (jnp.minimum(j * tn // section, 1), i % seq_tiles, 0))
        out_spec = pl.BlockSpec((tm, tn), lambda i, j: (i, j))
        out_shape = jax.ShapeDtypeStruct((m, n_out), BF16)
    else:
        def residue_major(tab):
            t = tab.reshape(2, seq_tiles, tm // dilation, dilation, head_dim)
            return t.transpose(0, 1, 3, 2, 4).reshape(2, seq_tiles, tm, head_dim)

        cos_tab, sin_tab = residue_major(cos_tab), residue_major(sin_tab)
        tab_spec = pl.BlockSpec((1, 1, tm, head_dim),
                                lambda i, j: (jnp.minimum(j * tn // section, 1), i % seq_tiles, 0, 0))
        out_spec = pl.BlockSpec((1, dilation, tm // dilation, tn),
                                lambda i, j: (i // seq_tiles, 0, i % seq_tiles, j))
        out_shape = jax.ShapeDtypeStruct((batch, dilation, seq // dilation, n_out), BF16)
        if dilation > 1:
            scratch += [pltpu.VMEM((IN_PROJ_SLABS, IN_PROJ_ROWS, LANES), F32)] * 2
    return pl.pallas_call(
        functools.partial(_in_proj_kernel, head_dim=head_dim, rot_tiles=rot_tiles,
                          dilation=dilation),
        grid=(m // tm, n_out // tn),
        in_specs=[
            pl.BlockSpec((tm, k), lambda i, j: (i, 0)),
            pl.BlockSpec((1, k), lambda i, j: (0, 0)),
            pl.BlockSpec((None, k, tn), lambda i, j: (layer, 0, col_tile0 + j)),
            tab_spec,
            tab_spec,
        ],
        out_specs=out_spec,
        out_shape=out_shape,
        scratch_shapes=scratch,
        compiler_params=_params("parallel", "arbitrary"),
        name="in_proj" if dilation is None else f"in_proj_d{dilation}",
    )(h, gain.reshape(1, k), w, cos_tab, sin_tab)


A_SPAN = max(d for _, d in A_PATTERNS) * A_BLOCK
A_COMBINE_ROWS = 256


def _attn_kernel(*refs, heads):
    in_refs, out_ref, o_nat, lse_nat = refs[:-3], refs[-3], refs[-2], refs[-1]
    blk = A_BLOCK
    span = pl.program_id(1)
    qi = lax.broadcasted_iota(jnp.int32, (blk, 2 * blk), 0)
    kj = lax.broadcasted_iota(jnp.int32, (blk, 2 * blk), 1)

    for g, (window, dilation) in enumerate(A_PATTERNS):
        q_ref, kp_ref, kc_ref, vp_ref, vc_ref = in_refs[5 * g:5 * g + 5]
        n_back = window // dilation
        nq = A_SPAN // (dilation * blk)
        band = (kj >= qi + blk - n_back) & (kj <= qi + blk)
        band_first = (kj >= jnp.maximum(qi + blk - n_back, jnp.where(span > 0, 0, blk))) & (
            kj <= qi + blk)
        for r in range(dilation):
            for bq in range(nq):
                if dilation > 1:
                    rows = pl.ds(bq * blk * dilation + r, blk, stride=dilation)
                else:
                    rows = slice(bq * blk, (bq + 1) * blk)
                for h in range(heads):
                    hs = slice(h * A_HEAD_DIM, (h + 1) * A_HEAD_DIM)
                    q = q_ref[0, r, bq * blk:(bq + 1) * blk, hs]
                    if bq == 0:
                        k = jnp.concatenate([kp_ref[0, r, :, hs], kc_ref[0, r, 0:blk, hs]], axis=0)
                        v = jnp.concatenate([vp_ref[0, r, :, hs], vc_ref[0, r, 0:blk, hs]], axis=0)
                        valid = band_first
                    else:
                        k = kc_ref[0, r, (bq - 1) * blk:(bq + 1) * blk, hs]
                        v = vc_ref[0, r, (bq - 1) * blk:(bq + 1) * blk, hs]
                        valid = band
                    s = lax.dot_general(q, k, (((1,), (1,)), ((), ())),
                                        preferred_element_type=F32)
                    s = jnp.where(valid, s, -jnp.inf)
                    mx = jnp.max(s, axis=-1, keepdims=True)
                    p = jnp.exp(s - mx)
                    l = jnp.sum(p, axis=-1, keepdims=True)
                    o = jnp.dot(p.astype(BF16), v, preferred_element_type=F32)
                    o_nat[g * heads + h, rows, :] = o / l
                    lse_nat[g * heads + h, rows, :] = jnp.broadcast_to(mx + jnp.log(l), (blk, LANES))

    n_groups = len(A_PATTERNS)
    for h in range(heads):
        def combine(c, carry, h=h):
            rs = pl.ds(pl.multiple_of(c * A_COMBINE_ROWS, A_COMBINE_ROWS), A_COMBINE_ROWS)
            lses = [lse_nat[g * heads + h, rs, :] for g in range(n_groups)]
            mx = functools.reduce(jnp.maximum, lses)
            es = [jnp.exp(l - mx) for l in lses]
            num = sum(e * o_nat[g * heads + h, rs, :] for g, e in enumerate(es))
            out_ref[rs, h * A_HEAD_DIM:(h + 1) * A_HEAD_DIM] = (num / sum(es)).astype(out_ref.dtype)
            return carry
        lax.fori_loop(0, A_SPAN // A_COMBINE_ROWS, combine, 0)


def _attention(projs, *, batch, seq, heads=2):
    width = heads * A_HEAD_DIM
    sec_blocks = A_WIDTH // width
    n_spans = seq // A_SPAN
    in_specs = []
    for (window, dilation) in A_PATTERNS:
        sub_rows = A_SPAN // dilation
        nq = sub_rows // A_BLOCK

        def cur(t, dilation=dilation, sub_rows=sub_rows):
            return pl.BlockSpec((1, dilation, sub_rows, width),
                                lambda b, s, hh: (b, 0, s, t * sec_blocks + hh))

        def prev(t, dilation=dilation, nq=nq):
            return pl.BlockSpec((1, dilation, A_BLOCK, width),
                                lambda b, s, hh: (b, 0, jnp.maximum(s * nq - 1, 0),
                                                  t * sec_blocks + hh))

        in_specs += [cur(0), prev(1), cur(1), prev(2), cur(2)]
    n_slabs = len(A_PATTERNS) * heads
    return pl.pallas_call(
        functools.partial(_attn_kernel, heads=heads),
        grid=(batch, n_spans, sec_blocks),
        in_specs=in_specs,
        out_specs=pl.BlockSpec((A_SPAN, width), lambda b, s, hh: (b * n_spans + s, hh)),
        out_shape=jax.ShapeDtypeStruct((batch * seq, A_WIDTH), BF16),
        scratch_shapes=[pltpu.VMEM((n_slabs, A_SPAN, LANES), F32),
                        pltpu.VMEM((n_slabs, A_SPAN, LANES), F32)],
        compiler_params=_params("parallel", "parallel", "parallel"),
        name="attention",
    )(*[p for p in projs for _ in range(5)])


def _retention_kernel(q_ref, k_ref, v_ref, g_ref, decay_ref, qdec_ref, kdec_ref, cdec_ref,
                      o_ref, state_ref, *, chunks):
    c_len = R_CHUNK

    @pl.when(pl.program_id(2) == 0)
    def _():
        state_ref[...] = jnp.zeros_like(state_ref)

    decay = decay_ref[0]
    qdec = qdec_ref[0]
    kdec = kdec_ref[0]
    cdec = cdec_ref[0]
    for c in range(chunks):
        rows = slice(c * c_len, (c + 1) * c_len)
        q = q_ref[rows, :]
        k = k_ref[rows, :]
        v = v_ref[rows, :]
        scores = lax.dot_general(q, k, (((1,), (1,)), ((), ())), preferred_element_type=F32) * decay
        y = jnp.dot(scores.astype(BF16), v, preferred_element_type=F32)
        state = state_ref[...]
        y = y + jnp.dot(q, state.astype(BF16), preferred_element_type=F32) * qdec
        k_dec = (k.astype(F32) * kdec).astype(BF16)
        state_ref[...] = state * cdec + lax.dot_general(
            k_dec, v, (((0,), (0,)), ((), ())), preferred_element_type=F32)
        mu = jnp.mean(y, axis=-1, keepdims=True)
        yc = y - mu
        var = jnp.mean(yc * yc, axis=-1, keepdims=True)
        yn = yc * lax.rsqrt(var + EPS)
        g = g_ref[rows, :].astype(F32)
        o_ref[rows, :] = (g / (1.0 + jnp.exp(-g)) * yn).astype(o_ref.dtype)


def _retention_tables():
    heads = R_HEADS
    c_len = R_CHUNK
    log_gamma = jnp.log1p(-jnp.exp2(-5.0 - jnp.arange(heads, dtype=F32)))
    idx = jnp.arange(c_len, dtype=F32)
    rel = idx[:, None] - idx[None, :]
    decay = jnp.where(rel >= 0, jnp.exp(log_gamma[:, None, None] * jnp.maximum(rel, 0.0)), 0.0)
    q_decay = jnp.exp(log_gamma[None, :] * (idx[:, None] + 1.0))
    k_decay = jnp.exp(log_gamma[None, :] * (c_len - 1.0 - idx[:, None]))
    chunk_decay = jnp.exp(log_gamma * c_len)
    qdec = jnp.broadcast_to(q_decay.T[:, :, None], (heads, c_len, R_V_DIM))
    kdec = jnp.broadcast_to(k_decay.T[:, :, None], (heads, c_len, R_QK_DIM))
    cdec = jnp.broadcast_to(chunk_decay[:, None, None], (heads, 1, R_V_DIM))
    return decay, qdec, kdec, cdec


def _retention(proj, *, batch, seq, chunks=8):
    m = proj.shape[0]
    rows = chunks * R_CHUNK
    steps = seq // rows
    decay, qdec, kdec, cdec = _retention_tables()
    k_off = R_QK_WIDTH // R_QK_DIM
    v_off = 2 * R_QK_WIDTH // R_V_DIM
    g_off = (2 * R_QK_WIDTH + R_V_WIDTH) // R_V_DIM

    def tok(off):
        return lambda b, h, t: (b * steps + t, off + h)

    def per_head(b, h, t):
        return (h, 0, 0)

    return pl.pallas_call(
        functools.partial(_retention_kernel, chunks=chunks),
        grid=(batch, R_HEADS, steps),
        in_specs=[
            pl.BlockSpec((rows, R_QK_DIM), tok(0)),
            pl.BlockSpec((rows, R_QK_DIM), tok(k_off)),
            pl.BlockSpec((rows, R_V_DIM), tok(v_off)),
            pl.BlockSpec((rows, R_V_DIM), tok(g_off)),
            pl.BlockSpec((1, R_CHUNK, R_CHUNK), per_head),
            pl.BlockSpec((1, R_CHUNK, R_V_DIM), per_head),
            pl.BlockSpec((1, R_CHUNK, R_QK_DIM), per_head),
            pl.BlockSpec((1, 1, R_V_DIM), per_head),
        ],
        out_specs=pl.BlockSpec((rows, R_V_DIM), tok(0)),
        out_shape=jax.ShapeDtypeStruct((m, R_V_WIDTH), BF16),
        scratch_shapes=[pltpu.VMEM((R_QK_DIM, R_V_DIM), F32)],
        compiler_params=_params("parallel", "parallel", "arbitrary"),
        name="retention",
    )(proj, proj, proj, proj, decay, qdec, kdec, cdec)


def _ffn_up_kernel(h_ref, halo_ref, g_ref, wg_ref, wu_ref, cwg_ref, cwu_ref, cbg_ref, cbu_ref,
                   o_ref, hn_ref, ug_ref, uu_ref, *, tiles_per_seq):
    tm = h_ref.shape[0]
    pad = halo_ref.shape[0]
    tn = o_ref.shape[1]

    @pl.when(pl.program_id(1) == 0)
    def _():
        at_seq_start = pl.program_id(0) % tiles_per_seq == 0
        halo = _rms_scale(halo_ref[...], g_ref[...])
        hn_ref[0:pad, :] = jnp.where(at_seq_start, 0.0, halo).astype(hn_ref.dtype)
        _rms_rows_to(hn_ref, pad, h_ref, g_ref[...])

    for c in range(tn // MXU_COLS):
        cols = slice(c * MXU_COLS, (c + 1) * MXU_COLS)
        slot = c % ug_ref.shape[0]

        def conv(u_ref, w_ref, cw_ref, cb_ref):
            u = u_ref.at[slot]
            u[...] = jnp.dot(hn_ref[...], w_ref[:, cols], preferred_element_type=F32)
            cw = cw_ref[:, cols]
            y = cb_ref[:, cols] + cw[CONV_WIDTH - 1:CONV_WIDTH] * u[pad:pad + tm, :]
            for back in range(1, CONV_WIDTH):
                tap = CONV_WIDTH - 1 - back
                y = y + cw[tap:tap + 1] * u[pad - back:pad - back + tm, :]
            return y

        gate = conv(ug_ref, wg_ref, cwg_ref, cbg_ref)
        up = conv(uu_ref, wu_ref, cwu_ref, cbu_ref)
        o_ref[:, cols] = (gate / (1.0 + jnp.exp(-gate)) * up).astype(o_ref.dtype)


def _ffn_up(h, gain, w_up, conv_w, conv_b, *, layer, seq, tm=1024, tn=512):
    m, k = h.shape
    pad = BF16_ROWS_PER_TILE
    n_tiles = D_FF // tn
    halo_blocks = tm // pad

    def halo_map(i, j):
        return (jnp.maximum(i * halo_blocks - 1, 0), 0)

    conv_b = conv_b.reshape(1, 2 * D_FF)
    return pl.pallas_call(
        functools.partial(_ffn_up_kernel, tiles_per_seq=seq // tm),
        grid=(m // tm, n_tiles),
        in_specs=[
            pl.BlockSpec((tm, k), lambda i, j: (i, 0)),
            pl.BlockSpec((pad, k), halo_map),
            pl.BlockSpec((1, k), lambda i, j: (0, 0)),
            pl.BlockSpec((None, k, tn), lambda i, j: (layer, 0, j)),
            pl.BlockSpec((None, k, tn), lambda i, j: (layer, 0, n_tiles + j)),
            pl.BlockSpec((CONV_WIDTH, tn), lambda i, j: (0, j)),
            pl.BlockSpec((CONV_WIDTH, tn), lambda i, j: (0, n_tiles + j)),
            pl.BlockSpec((1, tn), lambda i, j: (0, j)),
            pl.BlockSpec((1, tn), lambda i, j: (0, n_tiles + j)),
        ],
        out_specs=pl.BlockSpec((tm, tn), lambda i, j: (i, j)),
        out_shape=jax.ShapeDtypeStruct((m, D_FF), BF16),
        scratch_shapes=[pltpu.VMEM((pad + tm, k), BF16),
                        pltpu.VMEM((2, pad + tm, MXU_COLS), F32),
                        pltpu.VMEM((2, pad + tm, MXU_COLS), F32)],
        compiler_params=_params("parallel", "arbitrary"),
        name="ffn_up",
    )(h, h, gain.reshape(1, k), w_up, w_up, conv_w, conv_w, conv_b, conv_b)


def _out_proj_kernel(a_ref, w_ref, h_ref, o_ref):
    for c in range(o_ref.shape[1] // MXU_COLS):
        cols = slice(c * MXU_COLS, (c + 1) * MXU_COLS)
        o_ref[:, cols] = h_ref[:, cols] + jnp.dot(a_ref[...], w_ref[:, cols],
                                                  preferred_element_type=F32)


def _out_proj(a, w, h, *, layer, tm=1024, tn=512):
    m, k = a.shape
    d = w.shape[2]
    tile = pl.BlockSpec((tm, tn), lambda i, j: (i, j))
    return pl.pallas_call(
        _out_proj_kernel,
        grid=(m // tm, d // tn),
        in_specs=[
            pl.BlockSpec((tm, k), lambda i, j: (i, 0)),
            pl.BlockSpec((None, k, tn), lambda i, j: (layer, 0, j)),
            tile,
        ],
        out_specs=tile,
        out_shape=jax.ShapeDtypeStruct((m, d), F32),
        compiler_params=_params("parallel", "arbitrary"),
        name="out_proj",
    )(a, w, h)


def _rotary_tables(seq, inv_freq, q_scale, k_scale):
    ang = jnp.arange(seq, dtype=F32)[:, None] * inv_freq[None, :]
    cos, sin = jnp.cos(ang), jnp.sin(ang)
    cos_full = jnp.concatenate([cos, cos], axis=-1)
    sin_signed = jnp.concatenate([-sin, sin], axis=-1)
    scales = jnp.array([q_scale, k_scale], F32)[:, None, None]
    return cos_full[None] * scales, sin_signed[None] * scales


def kernel(x, norm_mix, norm_ffn, norm_final, w_in_attn, w_out_attn, w_in_ret, w_out_ret,
           w_up, conv_w, conv_b, w_down):
    batch, seq, d = x.shape
    depth = norm_mix.shape[0]
    inv_freq_a = ROPE_THETA ** (-jnp.arange(0, A_HEAD_DIM, 2, dtype=F32) / A_HEAD_DIM)
    cos_a, sin_a = _rotary_tables(seq, inv_freq_a, A_HEAD_DIM ** -0.5, 1.0)
    inv_freq_r = ROPE_THETA ** (-jnp.linspace(0.0, 1.0, R_QK_DIM // 2, dtype=F32))
    cos_r, sin_r = _rotary_tables(seq, inv_freq_r, 1.0, R_QK_DIM ** -0.5)

    w_in_attn, w_out_attn, w_in_ret, w_out_ret, w_up, w_down = (
        w.astype(BF16) for w in (w_in_attn, w_out_attn, w_in_ret, w_out_ret, w_up, w_down))

    h = x.reshape(batch * seq, d)
    for i in range(depth):
        li = i // 2
        if i % 2 == 0:
            group_cols = w_in_attn.shape[2] // A_N_GROUPS
            projs = [
                _in_proj(h, norm_mix[i], w_in_attn, cos_a, sin_a, layer=li, batch=batch, seq=seq,
                         head_dim=A_HEAD_DIM, n_out=group_cols, col_offset=g * group_cols,
                         dilation=dilation)
                for g, (_, dilation) in enumerate(A_PATTERNS)
            ]
            mixed = _attention(projs, batch=batch, seq=seq)
            h = _out_proj(mixed, w_out_attn, h, layer=li, tn=1024)
        else:
            proj = _in_proj(h, norm_mix[i], w_in_ret, cos_r, sin_r, layer=li, batch=batch, seq=seq,
                            head_dim=R_QK_DIM, n_out=w_in_ret.shape[2])
            mixed = _retention(proj, batch=batch, seq=seq)
            h = _out_proj(mixed, w_out_ret, h, layer=li)
        act = _ffn_up(h, norm_ffn[i], w_up, conv_w[i], conv_b[i], layer=i, seq=seq)
        h = _out_proj(act, w_down, h, layer=i)
    return _rmsnorm(h, norm_final).reshape(batch, seq, d)
```
